```python
import jax, jax.numpy as jnp
from jax import lax
import numpy as np

D_MODEL = 1024
BATCH = 2
SEQ = 8192
DEPTH = 1

POOL_WINDOWS = (2, 4, 8, 16)
POOL_GROUPS = len(POOL_WINDOWS)
POOL_WIDTH = D_MODEL // 2
POOL_GROUP_DIM = POOL_WIDTH // POOL_GROUPS
ATTN_HEADS = 8
HEAD_DIM = 64
ATTN_WIDTH = ATTN_HEADS * HEAD_DIM
MOBA_BLOCK = 256
MOBA_TOPK = 3
MOBA_Q_CHUNK = 64
ROPE_THETA = 500000.0
ROT_DIM = HEAD_DIM // 4
N_BRANCHES = 2
IN_PROJ_WIDTH = POOL_WIDTH + 3 * ATTN_WIDTH + N_BRANCHES * D_MODEL
PEER_HEADS = 8
PEER_NKEYS = 128
PEER_EXPERTS = PEER_NKEYS * PEER_NKEYS
PEER_QDIM = 256
PEER_HALF = PEER_QDIM // 2
PEER_TOPK = 16
PEER_TOKEN_CHUNK = 128
EPS = 1e-6
NEG = -1e30

kernel_name = "hybrid_pool_moba_peer_adaln"


def rmsnorm(x, g):
    xf = x.astype(jnp.float32)
    y = xf * lax.rsqrt(jnp.mean(xf * xf, axis=-1, keepdims=True) + EPS)
    return (y * g.astype(jnp.float32)).astype(x.dtype)


def modulate(x, g, shift, scale):
    return rmsnorm(x, g) * (1 + scale[:, None, :]) + shift[:, None, :]


def rope_partial(x, pos):
    half = ROT_DIM // 2
    inv = ROPE_THETA ** (-jnp.arange(half, dtype=jnp.float32) / half)
    ang = pos.astype(jnp.float32)[:, None] * inv[None, :]
    cos = jnp.cos(ang).astype(x.dtype)
    sin = jnp.sin(ang).astype(x.dtype)
    x1, x2, rest = x[..., :half], x[..., half:ROT_DIM], x[..., ROT_DIM:]
    return jnp.concatenate([x1 * cos - x2 * sin, x1 * sin + x2 * cos, rest], axis=-1)


def pool_mixer(u, pool_w, pool_scale):
    B, S, _ = u.shape
    uf = u.astype(jnp.float32)
    cs = jnp.cumsum(uf, axis=1)
    cs = jnp.concatenate([jnp.zeros_like(cs[:, :1]), cs], axis=1)
    pos = jnp.arange(S)
    groups = []
    for g, w in enumerate(POOL_WINDOWS):
        sl = slice(g * POOL_GROUP_DIM, (g + 1) * POOL_GROUP_DIM)
        lo = jnp.maximum(pos + 1 - w, 0)
        win_sum = cs[:, 1:, sl] - cs[:, lo, sl]
        cnt = jnp.minimum(pos + 1, w).astype(jnp.float32)[None, :, None]
        groups.append(win_sum / cnt)
    pooled = jnp.stack(groups, axis=2)
    diff = pooled - uf.reshape(B, S, POOL_GROUPS, POOL_GROUP_DIM)
    mixed = jnp.einsum('bsgi,gio->bsgo', diff, pool_w.astype(jnp.float32))
    mixed = mixed.reshape(B, S, POOL_WIDTH) * pool_scale.astype(jnp.float32)
    return mixed.astype(u.dtype)


def moba_attention(q, k, v):
    B, H, S, dh = q.shape
    n_blocks = -(-S // MOBA_BLOCK)
    pad = n_blocks * MOBA_BLOCK - S
    kp = jnp.pad(k, ((0, 0), (0, 0), (0, pad), (0, 0)))
    vp = jnp.pad(v, ((0, 0), (0, 0), (0, pad), (0, 0)))
    k_blocks = kp.reshape(B, H, n_blocks, MOBA_BLOCK, dh)
    v_blocks = vp.reshape(B, H, n_blocks, MOBA_BLOCK, dh)
    k_mean = jnp.mean(k_blocks.astype(jnp.float32), axis=3)
    topk = min(MOBA_TOPK, n_blocks)
    scale = HEAD_DIM ** -0.5
    bi = jnp.arange(B)[:, None, None, None]
    hi = jnp.arange(H)[None, :, None, None]
    blk_ids = jnp.arange(n_blocks)
    key_off = jnp.arange(MOBA_BLOCK)

    def chunk(start):
        q_c = lax.dynamic_slice_in_dim(q, start, MOBA_Q_CHUNK, axis=2)
        q_pos = start + jnp.arange(MOBA_Q_CHUNK)
        q_blk = q_pos // MOBA_BLOCK
        gate = jnp.einsum('bhcd,bhnd->bhcn', q_c.astype(jnp.float32), k_mean)
        past = blk_ids[None, :] < q_blk[:, None]
        gate = jnp.where(past, gate, -jnp.inf)
        _, sel = lax.top_k(gate, topk)
        sel_valid = sel < q_blk[:, None]
        k_sel = k_blocks[bi, hi, sel]
        v_sel = v_blocks[bi, hi, sel]
        s_past = jnp.einsum('bhcd,bhcjld->bhcjl', q_c, k_sel).astype(jnp.float32) * scale
        s_past = jnp.where(sel_valid[..., None], s_past, NEG)
        own = start // MOBA_BLOCK
        k_own = lax.dynamic_index_in_dim(k_blocks, own, axis=2, keepdims=False)
        v_own = lax.dynamic_index_in_dim(v_blocks, own, axis=2, keepdims=False)
        s_own = jnp.einsum('bhcd,bhld->bhcl', q_c, k_own).astype(jnp.float32) * scale
        k_pos = own * MOBA_BLOCK + key_off
        s_own = jnp.where(k_pos[None, :] <= q_pos[:, None], s_own, NEG)
        s = jnp.concatenate([s_past.reshape(B, H, MOBA_Q_CHUNK, topk * MOBA_BLOCK), s_own], axis=-1)
        p = jax.nn.softmax(s, axis=-1)
        p_past = p[..., :topk * MOBA_BLOCK].reshape(B, H, MOBA_Q_CHUNK, topk, MOBA_BLOCK).astype(v.dtype)
        p_own = p[..., topk * MOBA_BLOCK:].astype(v.dtype)
        return (jnp.einsum('bhcjl,bhcjld->bhcd', p_past, v_sel)
                + jnp.einsum('bhcl,bhld->bhcd', p_own, v_own))

    starts = jnp.arange(S // MOBA_Q_CHUNK) * MOBA_Q_CHUNK
    out = lax.map(chunk, starts)
    return out.transpose(1, 2, 0, 3, 4).reshape(B, H, S, dh)


def peer_ffn(h, w_q, sub_keys, u_tab, v_tab):
    B, S, D = h.shape
    T = B * S
    hf = h.reshape(T, D)
    q = (hf @ w_q).reshape(T, PEER_HEADS, 2, PEER_HALF)
    s = jnp.einsum('thpd,hpnd->thpn', q, sub_keys).astype(jnp.float32)
    sv, si = lax.top_k(s, PEER_TOPK)
    cand = sv[:, :, 0, :, None] + sv[:, :, 1, None, :]
    cand_idx = si[:, :, 0, :, None] * PEER_NKEYS + si[:, :, 1, None, :]
    cv, ci = lax.top_k(cand.reshape(T, PEER_HEADS, PEER_TOPK * PEER_TOPK), PEER_TOPK)
    expert = jnp.take_along_axis(cand_idx.reshape(T, PEER_HEADS, PEER_TOPK * PEER_TOPK), ci, axis=-1)
    g = jax.nn.softmax(cv, axis=-1)
    nc = T // PEER_TOKEN_CHUNK

    def chunk(args):
        x_c, e_c, g_c = args
        u = u_tab[e_c]
        vv = v_tab[e_c]
        a = jax.nn.gelu(jnp.einsum('cd,chkd->chk', x_c, u).astype(jnp.float32), approximate=False)
        return jnp.einsum('chk,chkd->cd', (g_c * a).astype(vv.dtype), vv)

    out = lax.map(chunk, (hf.reshape(nc, PEER_TOKEN_CHUNK, D),
                          expert.reshape(nc, PEER_TOKEN_CHUNK, PEER_HEADS, PEER_TOPK),
                          g.reshape(nc, PEER_TOKEN_CHUNK, PEER_HEADS, PEER_TOPK)))
    return out.reshape(B, S, D).astype(h.dtype)


def setup_inputs(seed: int = 0) -> dict:
    key = jax.random.key(seed)
    ks = jax.random.split(key, 20)
    D = D_MODEL
    nrm = lambda k, shape, s: jax.random.normal(k, shape, jnp.float32) * s
    return {
        "x": nrm(ks[0], (BATCH, SEQ, D), 1.0),
        "c": nrm(ks[1], (BATCH, D), 1.0),
        "w_ada": nrm(ks[2], (DEPTH, D, 6 * D), 0.5 * D ** -0.5),
        "b_ada": nrm(ks[3], (DEPTH, 6 * D), 0.01),
        "norm_mix_g": 1.0 + nrm(ks[4], (DEPTH, D), 0.02),
        "w_in": nrm(ks[5], (DEPTH, D, IN_PROJ_WIDTH), D ** -0.5),
        "pool_w": nrm(ks[6], (DEPTH, POOL_GROUPS, POOL_GROUP_DIM, POOL_GROUP_DIM), POOL_GROUP_DIM ** -0.5),
        "pool_scale": 1.0 + nrm(ks[7], (DEPTH, POOL_WIDTH), 0.1),
        "w_branch_pool": nrm(ks[8], (DEPTH, POOL_WIDTH, D), POOL_WIDTH ** -0.5),
        "w_branch_attn": nrm(ks[9], (DEPTH, ATTN_WIDTH, D), ATTN_WIDTH ** -0.5),
        "w_out": nrm(ks[10], (DEPTH, D, D), D ** -0.5),
        "norm_ffn_g": 1.0 + nrm(ks[11], (DEPTH, D), 0.02),
        "peer_wq": nrm(ks[12], (DEPTH, D, PEER_HEADS * PEER_QDIM), D ** -0.5),
        "peer_sub_keys": nrm(ks[13], (DEPTH, PEER_HEADS, 2, PEER_NKEYS, PEER_HALF), PEER_HALF ** -0.5),
        "peer_u": nrm(ks[14], (DEPTH, PEER_EXPERTS, D), D ** -0.5),
        "peer_v": nrm(ks[15], (DEPTH, PEER_EXPERTS, D), 0.5),
        "norm_final_g": 1.0 + nrm(ks[16], (D,), 0.02),
    }


def reference(x, c, w_ada, b_ada, norm_mix_g, w_in, pool_w, pool_scale, w_branch_pool,
              w_branch_attn, w_out, norm_ffn_g, peer_wq, peer_sub_keys, peer_u, peer_v,
              norm_final_g):
    B, S, D = x.shape
    pos = jnp.arange(S)
    c_act = jax.nn.silu(c)
    for l in range(DEPTH):
        mod = c_act @ w_ada[l] + b_ada[l]
        shift1, scale1, gate1, shift2, scale2, gate2 = jnp.split(mod, 6, axis=-1)

        h = modulate(x, norm_mix_g[l], shift1, scale1)
        proj = h @ w_in[l]
        u_pool = proj[..., :POOL_WIDTH]
        qkv = proj[..., POOL_WIDTH:POOL_WIDTH + 3 * ATTN_WIDTH]
        gates = proj[..., POOL_WIDTH + 3 * ATTN_WIDTH:]
        qkv = qkv.reshape(B, S, 3, ATTN_HEADS, HEAD_DIM).transpose(2, 0, 3, 1, 4)
        q = rope_partial(qkv[0], pos)
        k = rope_partial(qkv[1], pos)
        attn = moba_attention(q, k, qkv[2]).transpose(0, 2, 1, 3).reshape(B, S, ATTN_WIDTH)
        pooled = pool_mixer(u_pool, pool_w[l], pool_scale[l])
        g_pool = jax.nn.sigmoid(gates[..., :D])
        g_attn = jax.nn.sigmoid(gates[..., D:])
        merged = g_pool * (pooled @ w_branch_pool[l]) + g_attn * (attn @ w_branch_attn[l])
        x = x + gate1[:, None, :] * (merged @ w_out[l])

        h2 = modulate(x, norm_ffn_g[l], shift2, scale2)
        x = x + gate2[:, None, :] * peer_ffn(h2, peer_wq[l], peer_sub_keys[l], peer_u[l], peer_v[l])
    return rmsnorm(x, norm_final_g)
```

```python
import functools
import math

import jax
import jax.numpy as jnp
import numpy as np
from jax import lax
from jax.experimental import pallas as pl
from jax.experimental.pallas import tpu as pltpu

F32 = jnp.float32
BF16 = jnp.bfloat16
HIGHEST = lax.Precision.HIGHEST

D_MODEL = 1024
POOL_WINDOWS = (2, 4, 8, 16)
POOL_WIDTH = 512
POOL_GROUP_DIM = 128
POOL_HALO = 16
ATTN_HEADS = 8
HEAD_DIM = 64
ATTN_WIDTH = 512
MOBA_BLOCK = 256
MOBA_TOPK = 3
ROPE_THETA = 500000.0
ROT_HALF = 8
PEER_HEADS = 8
PEER_NKEYS = 128
PEER_EXPERTS = PEER_NKEYS * PEER_NKEYS
PEER_HALF = 128
PEER_TOPK = 16
EPS = 1e-6

LANES = 128
HEAD_PAD = LANES
ATTN_PAD = ATTN_HEADS * HEAD_PAD
BIAS_LANE0 = HEAD_DIM
ONES_LANE = HEAD_DIM
MASK_BIG = 2.0 ** 100
CAUSAL_NEG = -1e30
NOT_MEMBER = 255.0
SQRT_HALF = float(np.sqrt(0.5).astype(np.float32))
VMEM_LIMIT = 56 * 1024 * 1024

IN_PROJ_PAD = POOL_WIDTH + 3 * ATTN_PAD + 2 * D_MODEL
Q_OFF = POOL_WIDTH
K_OFF = Q_OFF + ATTN_PAD
V_OFF = K_OFF + ATTN_PAD
G_OFF = V_OFF + ATTN_PAD


def _params(*semantics):
    return pltpu.CompilerParams(dimension_semantics=semantics, vmem_limit_bytes=VMEM_LIMIT)


def _const_spec(shape):
    nd = len(shape)
    return pl.BlockSpec(shape, lambda *_: (0,) * nd, pipeline_mode=pl.Buffered(1))


def _rms_modulate(x, g, shift, scale):
    ms = jnp.mean(x * x, axis=-1, keepdims=True)
    return (x * lax.rsqrt(ms + EPS) * g) * (1.0 + scale) + shift


def _ada_kernel(c_ref, w_ref, b_ref, o_ref):
    c = c_ref[...]
    act = c / (1.0 + jnp.exp(-c))
    o_ref[...] = jnp.dot(act, w_ref[...], precision=HIGHEST,
                         preferred_element_type=F32) + b_ref[...]


def _ada(c8, w, b):
    n = w.shape[1]
    tn = 1536
    return pl.pallas_call(
        _ada_kernel,
        grid=(n // tn,),
        in_specs=[pl.BlockSpec((8, D_MODEL), lambda j: (0, 0)),
                  pl.BlockSpec((D_MODEL, tn), lambda j: (0, j)),
                  pl.BlockSpec((1, tn), lambda j: (0, j))],
        out_specs=pl.BlockSpec((8, tn), lambda j: (0, j)),
        out_shape=jax.ShapeDtypeStruct((8, n), F32),
        compiler_params=_params("arbitrary"),
        name="ada_mod",
    )(c8, w, b)


def _inproj_kernel(x_ref, sh_ref, sc_ref, g_ref, w_ref, cos_ref, sa_ref, sb_ref,
                   u_ref, q_ref, k_ref, v_ref, gt_ref):
    h = _rms_modulate(x_ref[...], g_ref[...], sh_ref[0], sc_ref[0]).astype(BF16)

    def proj(off, width):
        return jnp.dot(h, w_ref[:, off:off + width], preferred_element_type=F32)

    u_ref[...] = proj(0, POOL_WIDTH)
    cos, sa, sb = cos_ref[...], sa_ref[...], sb_ref[...]

    def rope(t):
        return t * cos + pltpu.roll(t, LANES - ROT_HALF, 1) * sa + pltpu.roll(t, ROT_HALF, 1) * sb

    for hd in range(ATTN_HEADS):
        sl = slice(hd * HEAD_PAD, (hd + 1) * HEAD_PAD)
        q = rope(proj(Q_OFF + hd * HEAD_PAD, HEAD_PAD))
        q_ref[:, sl] = (q * (HEAD_DIM ** -0.5)).astype(BF16)
        k_ref[:, sl] = rope(proj(K_OFF + hd * HEAD_PAD, HEAD_PAD)).astype(BF16)
    v_ref[...] = proj(V_OFF, ATTN_PAD).astype(BF16)
    gt_ref[...] = 1.0 / (1.0 + jnp.exp(-proj(G_OFF, 2 * D_MODEL)))


def _inproj(x2, shift, scale, g, w_p, cos_t, sa_t, sb_t, *, seq, tm):
    t = x2.shape[0]
    tpb = seq // tm
    row = lambda i: (i, 0)
    per_batch = lambda i: (i // tpb, 0, 0)
    per_pos = lambda i: (i % tpb, 0)
    return pl.pallas_call(
        _inproj_kernel,
        grid=(t // tm,),
        in_specs=[pl.BlockSpec((tm, D_MODEL), row),
                  pl.BlockSpec((1, 1, D_MODEL), per_batch),
                  pl.BlockSpec((1, 1, D_MODEL), per_batch),
                  _const_spec((1, D_MODEL)),
                  _const_spec((D_MODEL, IN_PROJ_PAD)),
                  pl.BlockSpec((tm, LANES), per_pos),
                  pl.BlockSpec((tm, LANES), per_pos),
                  pl.BlockSpec((tm, LANES), per_pos)],
        out_specs=[pl.BlockSpec((tm, POOL_WIDTH), row),
                   pl.BlockSpec((tm, ATTN_PAD), row),
                   pl.BlockSpec((tm, ATTN_PAD), row),
                   pl.BlockSpec((tm, ATTN_PAD), row),
                   pl.BlockSpec((tm, 2 * D_MODEL), row)],
        out_shape=[jax.ShapeDtypeStruct((t, POOL_WIDTH), F32),
                   jax.ShapeDtypeStruct((t, ATTN_PAD), BF16),
                   jax.ShapeDtypeStruct((t, ATTN_PAD), BF16),
                   jax.ShapeDtypeStruct((t, ATTN_PAD), BF16),
                   jax.ShapeDtypeStruct((t, 2 * D_MODEL), F32)],
        compiler_params=_params("arbitrary"),
        name="in_proj",
    )(x2, shift, scale, g, w_p, cos_t, sa_t, sb_t)


def _moba_kernel(q_ref, k_ref, v_ref, bsel_ref, o_ref, qa_scr, ka_scr, va_scr, *, n_blocks):
    qi = pl.program_id(2)
    blk = MOBA_BLOCK
    lane = lax.broadcasted_iota(jnp.int32, (blk, LANES), 1)
    lane_f = lane.astype(F32)

    @pl.when(qi == 0)
    def _prepare():
        kms = jnp.dot(bsel_ref[...], k_ref[...], preferred_element_type=F32)

        def per_block(nb, carry):
            rows = pl.ds(pl.multiple_of(nb * blk, blk), blk)
            q = q_ref[rows, :]
            gate = lax.dot_general(q.astype(F32), kms, (((1,), (1,)), ((), ())),
                                   precision=HIGHEST, preferred_element_type=F32)
            valid = (lane >= BIAS_LANE0) & (lane < BIAS_LANE0 + nb)
            work = jnp.where(valid, gate, -jnp.inf)
            chosen = jnp.zeros((blk, LANES), F32)
            for _ in range(MOBA_TOPK):
                m = jnp.max(work, axis=1, keepdims=True)
                idx = jnp.min(jnp.where(work == m, lane_f, 2.0 * LANES), axis=1, keepdims=True)
                hit = (lane_f == idx) & valid
                chosen = jnp.where(hit, 1.0, chosen)
                work = jnp.where(hit, -jnp.inf, work)
            bias_lanes = (lane >= BIAS_LANE0) & (lane < BIAS_LANE0 + n_blocks)
            masked = bias_lanes & (chosen == 0.0) & (lane != BIAS_LANE0 + nb)
            qa_scr[rows, :] = jnp.where(masked, -MASK_BIG, q.astype(F32)).astype(BF16)
            onehot = jnp.where(lane == BIAS_LANE0 + nb, 1.0, 0.0)
            ka_scr[rows, :] = jnp.where(lane < HEAD_DIM, k_ref[rows, :].astype(F32),
                                        onehot).astype(BF16)
            va_scr[rows, :] = jnp.where(lane == ONES_LANE, 1.0,
                                        v_ref[rows, :].astype(F32)).astype(BF16)
            return carry

        lax.fori_loop(0, n_blocks, per_block, 0)

    nt = (((1,), (1,)), ((), ()))
    own = pl.ds(pl.multiple_of(qi * blk, blk), blk)
    qa = qa_scr[own, :]
    s = lax.dot_general(qa, ka_scr[own, :], nt, preferred_element_type=F32)
    r_i = lax.broadcasted_iota(jnp.int32, (blk, blk), 0)
    c_i = lax.broadcasted_iota(jnp.int32, (blk, blk), 1)
    s = jnp.where(c_i <= r_i, s, CAUSAL_NEG)
    m0 = jnp.max(s, axis=1, keepdims=True)
    p = jnp.exp(s - m0)
    acc0 = jnp.dot(p.astype(BF16), va_scr[own, :], preferred_element_type=F32)

    def past(j, carry):
        m, acc = carry
        rows = pl.ds(pl.multiple_of(j * blk, blk), blk)
        sj = lax.dot_general(qa, ka_scr[rows, :], nt, preferred_element_type=F32)
        m_new = jnp.maximum(m, jnp.max(sj, axis=1, keepdims=True))
        pj = jnp.exp(sj - m_new)
        acc = acc * jnp.exp(m - m_new) + jnp.dot(pj.astype(BF16), va_scr[rows, :],
                                                 preferred_element_type=F32)
        return m_new, acc

    _, acc = lax.fori_loop(0, qi, past, (m0, acc0))
    o_ref[...] = (acc / acc[:, ONES_LANE:ONES_LANE + 1]).astype(BF16)


def _moba(qp, kp, vp, bsel, *, batch, seq):
    n_blocks = seq // MOBA_BLOCK
    head = lambda b, h, i: (b, h)
    return pl.pallas_call(
        functools.partial(_moba_kernel, n_blocks=n_blocks),
        grid=(batch, ATTN_HEADS, n_blocks),
        in_specs=[pl.BlockSpec((seq, HEAD_PAD), head),
                  pl.BlockSpec((seq, HEAD_PAD), head),
                  pl.BlockSpec((seq, HEAD_PAD), head),
                  _const_spec((LANES, seq))],
        out_specs=pl.BlockSpec((MOBA_BLOCK, HEAD_PAD), lambda b, h, i: (b * n_blocks + i, h)),
        out_shape=jax.ShapeDtypeStruct((batch * seq, ATTN_PAD), BF16),
        scratch_shapes=[pltpu.VMEM((seq, HEAD_PAD), BF16)] * 3,
        compiler_params=_params("arbitrary", "arbitrary", "arbitrary"),
        name="moba_attn",
    )(qp, kp, vp, bsel)


def _merge_kernel(u_ref, uprev_ref, attn_ref, gt_ref, x_ref, poolw_ref, pscale_ref, wp_ref,
                  wa_ref, wo_ref, gate1_ref, g2_ref, sh2_ref, sc2_ref, x1_ref, h2t_ref,
                  *, tm, tpb):
    i = pl.program_id(0)
    first = (i % tpb) == 0
    pos = lax.broadcasted_iota(jnp.int32, (tm, LANES), 0) + (i % tpb) * tm
    mixed = []
    for g, w in enumerate(POOL_WINDOWS):
        sl = slice(g * POOL_GROUP_DIM, (g + 1) * POOL_GROUP_DIM)
        u = u_ref[:, sl]
        halo = jnp.where(first, 0.0, uprev_ref[:, sl])
        ext = jnp.concatenate([halo, u], axis=0)
        span = 1
        while span < w:
            ext = ext + pltpu.roll(ext, span, 0)
            span *= 2
        cnt = jnp.minimum(pos + 1, w).astype(F32)
        diff = ext[POOL_HALO:, :] / cnt - u
        m = jnp.dot(diff.astype(BF16), poolw_ref[g], preferred_element_type=F32)
        mixed.append((m * pscale_ref[:, sl]).astype(BF16))
    pooled = jnp.concatenate(mixed, axis=1)
    bp = jnp.dot(pooled, wp_ref[...], preferred_element_type=F32)
    ba = jnp.dot(attn_ref[...], wa_ref[...], preferred_element_type=F32)
    merged = gt_ref[:, :D_MODEL] * bp + gt_ref[:, D_MODEL:] * ba
    y = jnp.dot(merged.astype(BF16), wo_ref[...], preferred_element_type=F32)
    x1 = x_ref[...] + gate1_ref[0] * y
    x1_ref[...] = x1
    h2 = _rms_modulate(x1, g2_ref[...], sh2_ref[0], sc2_ref[0])
    h2t_ref[...] = h2.T.astype(BF16)


def _merge(u_pool, attn, gates, x2, poolw, pscale, wp, wa, wo, gate1, g2, shift2, scale2,
           *, seq, tm):
    t = x2.shape[0]
    tpb = seq // tm
    row = lambda i: (i, 0)
    per_batch = lambda i: (i // tpb, 0, 0)
    halo_blocks = tm // POOL_HALO
    return pl.pallas_call(
        functools.partial(_merge_kernel, tm=tm, tpb=tpb),
        grid=(t // tm,),
        in_specs=[pl.BlockSpec((tm, POOL_WIDTH), row),
                  pl.BlockSpec((POOL_HALO, POOL_WIDTH),
                               lambda i: (jnp.maximum(i * halo_blocks - 1, 0), 0)),
                  pl.BlockSpec((tm, ATTN_PAD), row),
                  pl.BlockSpec((tm, 2 * D_MODEL), row),
                  pl.BlockSpec((tm, D_MODEL), row),
                  _const_spec((len(POOL_WINDOWS), POOL_GROUP_DIM, POOL_GROUP_DIM)),
                  _const_spec((1, POOL_WIDTH)),
                  _const_spec((POOL_WIDTH, D_MODEL)),
                  _const_spec((ATTN_PAD, D_MODEL)),
                  _const_spec((D_MODEL, D_MODEL)),
                  pl.BlockSpec((1, 1, D_MODEL), per_batch),
                  _const_spec((1, D_MODEL)),
                  pl.BlockSpec((1, 1, D_MODEL), per_batch),
                  pl.BlockSpec((1, 1, D_MODEL), per_batch)],
        out_specs=[pl.BlockSpec((tm, D_MODEL), row),
                   pl.BlockSpec((D_MODEL, tm), lambda i: (0, i))],
        out_shape=[jax.ShapeDtypeStruct((t, D_MODEL), F32),
                   jax.ShapeDtypeStruct((D_MODEL, t), BF16)],
        compiler_params=_params("arbitrary"),
        name="mixer_merge",
    )(u_pool, u_pool, attn, gates, x2, poolw, pscale, wp, wa, wo, gate1, g2, shift2, scale2)


def _extract_top16(scores, index_f, count):
    width = scores.shape[1]
    slot = lax.broadcasted_iota(jnp.int32, (PEER_TOPK, width), 0)

    def body(r, carry):
        work, rank, vals = carry
        m = jnp.max(work, axis=0, keepdims=True)
        idx = jnp.min(jnp.where(work == m, index_f, float(count)), axis=0, keepdims=True)
        hit = index_f == idx
        rank = jnp.where(hit, r.astype(F32), rank)
        work = jnp.where(hit, -jnp.inf, work)
        vals = jnp.where(slot == r, m, vals)
        return work, rank, vals

    init = (scores, jnp.full(scores.shape, NOT_MEMBER, F32), jnp.zeros((PEER_TOPK, width), F32))
    _, rank, vals = lax.fori_loop(0, PEER_TOPK, body, init)
    return vals, rank


def _route_kernel(h2t_ref, wqt_ref, keys_ref, r2_ref, e1_ref, n_ref, c_ref, qt_scr, *, tm):
    qt_scr[...] = jnp.dot(wqt_ref[...], h2t_ref[...], preferred_element_type=F32)
    key_f = lax.broadcasted_iota(jnp.int32, (PEER_NKEYS, LANES), 0).astype(F32)
    pair_f = lax.broadcasted_iota(jnp.int32, (PEER_TOPK * PEER_TOPK, LANES), 0).astype(F32)
    slot = lax.broadcasted_iota(jnp.int32, (PEER_TOPK, LANES), 0)

    for cc in range(tm // LANES):
        cols = slice(cc * LANES, (cc + 1) * LANES)

        def per_head(h, carry, cols=cols):
            def half_scores(p):
                rows = pl.ds(pl.multiple_of((2 * h + p) * PEER_HALF, PEER_HALF), PEER_HALF)
                return jnp.dot(keys_ref[2 * h + p], qt_scr[rows, cols], precision=HIGHEST,
                               preferred_element_type=F32)

            s0 = half_scores(0)
            s1 = half_scores(1)
            a, rank1 = _extract_top16(s0, key_f, PEER_NKEYS)
            b, rank2 = _extract_top16(s1, key_f, PEER_NKEYS)
            cand = jnp.concatenate([a[r:r + 1, :] + b for r in range(PEER_TOPK)], axis=0)
            _, pick = _extract_top16(cand, pair_f, PEER_TOPK * PEER_TOPK)
            chosen = jnp.where(pick < NOT_MEMBER, 1.0, 0.0)
            e0 = jnp.exp(a - a[0:1, :])
            e1 = jnp.exp(b - b[0:1, :])
            n = jnp.zeros((PEER_TOPK, LANES), F32)
            pref = jnp.zeros((PEER_TOPK, LANES), F32)
            for r in range(PEER_TOPK):
                grp = chosen[r * PEER_TOPK:(r + 1) * PEER_TOPK, :]
                n = jnp.where(slot == r, jnp.sum(grp, axis=0, keepdims=True), n)
                pref = jnp.where(slot == r, jnp.sum(grp * e1, axis=0, keepdims=True), pref)
            z = jnp.sum(e0 * pref, axis=0, keepdims=True)
            n_dense = jnp.zeros((PEER_NKEYS, LANES), F32)
            for r in range(PEER_TOPK):
                n_dense = jnp.where(rank1 == float(r), n[r:r + 1, :], n_dense)
            r2_ref[h, :, cols] = rank2
            e1_ref[h, :, cols] = jnp.exp(s1 - b[0:1, :])
            n_ref[h, :, cols] = n_dense
            c_ref[h, :, cols] = jnp.exp(s0 - a[0:1, :]) / z
            return carry

        lax.fori_loop(0, PEER_HEADS, per_head, 0)


def _route(h2t, wqt, keys, *, tm):
    t = h2t.shape[1]
    dense = jax.ShapeDtypeStruct((PEER_HEADS, PEER_NKEYS, t), F32)
    dense_spec = pl.BlockSpec((PEER_HEADS, PEER_NKEYS, tm), lambda i: (0, 0, i))
    return pl.pallas_call(
        functools.partial(_route_kernel, tm=tm),
        grid=(t // tm,),
        in_specs=[pl.BlockSpec((D_MODEL, tm), lambda i: (0, i)),
                  _const_spec((2 * PEER_HEADS * PEER_HALF, D_MODEL)),
                  _const_spec((2 * PEER_HEADS, PEER_NKEYS, PEER_HALF))],
        out_specs=[dense_spec] * 4,
        out_shape=[dense] * 4,
        scratch_shapes=[pltpu.VMEM((2 * PEER_HEADS * PEER_HALF, tm), F32)],
        compiler_params=_params("arbitrary"),
        name="peer_route",
    )(h2t, wqt, keys)


def _experts_kernel(h2t_ref, u_ref, vt_ref, r2_ref, e1_ref, n_ref, c_ref, x1_ref, gate2_ref,
                    gf_ref, o_ref, acc_scr, p_scr, *, tm, sub_blocks):
    e = pl.program_id(1)

    @pl.when(e == 0)
    def _zero():
        acc_scr[...] = jnp.zeros_like(acc_scr)

    ht = h2t_ref[...]
    for jb in range(sub_blocks):
        rows = slice(jb * PEER_NKEYS, (jb + 1) * PEER_NKEYS)
        act = jnp.dot(u_ref[rows, :], ht, preferred_element_type=F32)
        for cc in range(tm // LANES):
            cols = slice(cc * LANES, (cc + 1) * LANES)
            a = act[:, cols]
            gelu = 0.5 * a * (1.0 + lax.erf(a * SQRT_HALF))
            w = jnp.zeros((PEER_NKEYS, LANES), F32)
            for h in range(PEER_HEADS):
                keep = r2_ref[h, :, cols] < n_ref[h, jb:jb + 1, cols]
                w = w + jnp.where(keep, e1_ref[h, :, cols] * c_ref[h, jb:jb + 1, cols], 0.0)
            p_scr[rows, cols] = (w * gelu).astype(BF16)
    acc_scr[...] += jnp.dot(vt_ref[...], p_scr[...], preferred_element_type=F32)

    @pl.when(e == pl.num_programs(1) - 1)
    def _finish():
        x2 = x1_ref[...] + gate2_ref[0] * acc_scr[...].T
        ms = jnp.mean(x2 * x2, axis=-1, keepdims=True)
        o_ref[...] = x2 * lax.rsqrt(ms + EPS) * gf_ref[...]


def _experts(h2t, u_b, vt_b, r2, e1, n, c, x1, gate2, gf, *, seq, tm, sub_blocks):
    t = h2t.shape[1]
    tpb = seq // tm
    eb = sub_blocks * PEER_NKEYS
    tok3 = lambda i, e: (0, 0, i)
    return pl.pallas_call(
        functools.partial(_experts_kernel, tm=tm, sub_blocks=sub_blocks),
        grid=(t // tm, PEER_EXPERTS // eb),
        in_specs=[pl.BlockSpec((D_MODEL, tm), lambda i, e: (0, i)),
                  pl.BlockSpec((eb, D_MODEL), lambda i, e: (e, 0)),
                  pl.BlockSpec((D_MODEL, eb), lambda i, e: (0, e)),
                  pl.BlockSpec((PEER_HEADS, PEER_NKEYS, tm), tok3),
                  pl.BlockSpec((PEER_HEADS, PEER_NKEYS, tm), tok3),
                  pl.BlockSpec((PEER_HEADS, sub_blocks, tm), lambda i, e: (0, e, i)),
                  pl.BlockSpec((PEER_HEADS, sub_blocks, tm), lambda i, e: (0, e, i)),
                  pl.BlockSpec((tm, D_MODEL), lambda i, e: (i, 0)),
                  pl.BlockSpec((1, 1, D_MODEL), lambda i, e: (i // tpb, 0, 0)),
                  pl.BlockSpec((1, D_MODEL), lambda i, e: (0, 0))],
        out_specs=pl.BlockSpec((tm, D_MODEL), lambda i, e: (i, 0)),
        out_shape=jax.ShapeDtypeStruct((t, D_MODEL), F32),
        scratch_shapes=[pltpu.VMEM((D_MODEL, tm), F32), pltpu.VMEM((eb, tm), BF16)],
        compiler_params=_params("arbitrary", "arbitrary"),
        name="peer_experts",
    )(h2t, u_b, vt_b, r2, e1, n, c, x1, gate2, gf)


def _pad_heads(w):
    d = w.shape[0]
    w = w.reshape(d, ATTN_HEADS, HEAD_DIM)
    return jnp.pad(w, ((0, 0), (0, 0), (0, HEAD_PAD - HEAD_DIM))).reshape(d, ATTN_PAD)


def _rope_tables(seq):
    inv = ROPE_THETA ** (-jnp.arange(ROT_HALF, dtype=F32) / ROT_HALF)
    ang = jnp.arange(seq).astype(F32)[:, None] * inv[None, :]
    cos, sin = jnp.cos(ang), jnp.sin(ang)
    z = lambda n: jnp.zeros((seq, n), F32)
    cos_t = jnp.concatenate([cos, cos, jnp.ones((seq, LANES - 2 * ROT_HALF), F32)], axis=1)
    sa_t = jnp.concatenate([-sin, z(LANES - ROT_HALF)], axis=1)
    sb_t = jnp.concatenate([z(ROT_HALF), sin, z(LANES - 2 * ROT_HALF)], axis=1)
    return cos_t, sa_t, sb_t


def _block_mean_rows(seq):
    r = jnp.arange(LANES)[:, None]
    s = jnp.arange(seq)[None, :]
    return jnp.where(r - BIAS_LANE0 == s // MOBA_BLOCK, 1.0 / MOBA_BLOCK, 0.0).astype(BF16)


def kernel(x, c, w_ada, b_ada, norm_mix_g, w_in, pool_w, pool_scale, w_branch_pool,
           w_branch_attn, w_out, norm_ffn_g, peer_wq, peer_sub_keys, peer_u, peer_v,
           norm_final_g):
    batch, seq, d = x.shape
    depth = w_ada.shape[0]
    assert d == D_MODEL and batch <= 8 and seq % 512 == 0
    assert seq // MOBA_BLOCK <= LANES - BIAS_LANE0
    assert depth == 1
    t = batch * seq
    x2 = x.reshape(t, d)
    c8 = jnp.pad(c, ((0, 8 - batch), (0, 0)))
    cos_t, sa_t, sb_t = _rope_tables(seq)
    bsel = _block_mean_rows(seq)
    row1 = lambda v: v.reshape(1, -1)
    per_batch = lambda v: v.reshape(batch, 1, d)

    for l in range(depth):
        mod = _ada(c8, w_ada[l], row1(b_ada[l]))[:batch]
        shift1, scale1, gate1, shift2, scale2, gate2 = [per_batch(m) for m in
                                                        jnp.split(mod, 6, axis=-1)]
        wl = w_in[l]
        w_p = jnp.concatenate(
            [wl[:, :POOL_WIDTH],
             _pad_heads(wl[:, POOL_WIDTH:POOL_WIDTH + ATTN_WIDTH]),
             _pad_heads(wl[:, POOL_WIDTH + ATTN_WIDTH:POOL_WIDTH + 2 * ATTN_WIDTH]),
             _pad_heads(wl[:, POOL_WIDTH + 2 * ATTN_WIDTH:POOL_WIDTH + 3 * ATTN_WIDTH]),
             wl[:, POOL_WIDTH + 3 * ATTN_WIDTH:]], axis=1).astype(BF16)
        u_pool, qp, kp, vp, gates = _inproj(x2, shift1, scale1, row1(norm_mix_g[l]), w_p,
                                            cos_t, sa_t, sb_t, seq=seq, tm=256)
        attn = _moba(qp, kp, vp, bsel, batch=batch, seq=seq)
        wa_p = _pad_heads(w_branch_attn[l].T).T.astype(BF16)
        x1, h2t = _merge(u_pool, attn, gates, x2, pool_w[l].astype(BF16), row1(pool_scale[l]),
                         w_branch_pool[l].astype(BF16), wa_p, w_out[l].astype(BF16), gate1,
                         row1(norm_ffn_g[l]), shift2, scale2, seq=seq, tm=256)
        keys = peer_sub_keys[l].reshape(2 * PEER_HEADS, PEER_NKEYS, PEER_HALF)
        r2, e1, n, cden = _route(h2t, peer_wq[l].T.astype(BF16), keys, tm=256)
        x2 = _experts(h2t, peer_u[l].astype(BF16), peer_v[l].T.astype(BF16), r2, e1, n, cden,
                      x1, gate2, row1(norm_final_g), seq=seq, tm=512, sub_blocks=8)
    return x2.reshape(batch, seq, d)
```

```python
import functools
import math

import jax
import jax.numpy as jnp
import numpy as np
from jax import lax
from jax.experimental import pallas as pl
from jax.experimental.pallas import tpu as pltpu

F32 = jnp.float32
BF16 = jnp.bfloat16
HIGHEST = lax.Precision.HIGHEST

D_MODEL = 1024
POOL_WINDOWS = (2, 4, 8, 16)
POOL_WIDTH = 512
POOL_GROUP_DIM = 128
POOL_HALO = 16
ATTN_HEADS = 8
HEAD_DIM = 64
ATTN_WIDTH = 512
MOBA_BLOCK = 256
MOBA_TOPK = 3
ROPE_THETA = 500000.0
ROT_HALF = 8
PEER_HEADS = 8
PEER_NKEYS = 128
PEER_EXPERTS = PEER_NKEYS * PEER_NKEYS
PEER_HALF = 128
PEER_TOPK = 16
EPS = 1e-6

LANES = 128
HEAD_PAD = LANES
ATTN_PAD = ATTN_HEADS * HEAD_PAD
BIAS_LANE0 = HEAD_DIM
ONES_LANE = HEAD_DIM
MASK_BIG = 2.0 ** 100
CAUSAL_NEG = -1e30
NOT_MEMBER = 255.0
SQRT_HALF = float(np.sqrt(0.5).astype(np.float32))
VMEM_LIMIT = 56 * 1024 * 1024

IN_PROJ_PAD = POOL_WIDTH + 3 * ATTN_PAD + 2 * D_MODEL
Q_OFF = POOL_WIDTH
K_OFF = Q_OFF + ATTN_PAD
V_OFF = K_OFF + ATTN_PAD
G_OFF = V_OFF + ATTN_PAD


def _params(*semantics):
    return pltpu.CompilerParams(dimension_semantics=semantics, vmem_limit_bytes=VMEM_LIMIT)


def _const_spec(shape):
    nd = len(shape)
    return pl.BlockSpec(shape, lambda *_: (0,) * nd, pipeline_mode=pl.Buffered(1))


def _rms_modulate(x, g, shift, scale):
    ms = jnp.mean(x * x, axis=-1, keepdims=True)
    return (x * lax.rsqrt(ms + EPS) * g) * (1.0 + scale) + shift


def _ada_kernel(c_ref, w_ref, b_ref, o_ref):
    c = c_ref[...]
    act = c / (1.0 + jnp.exp(-c))
    o_ref[...] = jnp.dot(act, w_ref[...], precision=HIGHEST,
                         preferred_element_type=F32) + b_ref[...]


def _ada(c8, w, b):
    n = w.shape[1]
    tn = 1536
    return pl.pallas_call(
        _ada_kernel,
        grid=(n // tn,),
        in_specs=[pl.BlockSpec((8, D_MODEL), lambda j: (0, 0)),
                  pl.BlockSpec((D_MODEL, tn), lambda j: (0, j)),
                  pl.BlockSpec((1, tn), lambda j: (0, j))],
        out_specs=pl.BlockSpec((8, tn), lambda j: (0, j)),
        out_shape=jax.ShapeDtypeStruct((8, n), F32),
        compiler_params=_params("arbitrary"),
        name="ada_mod",
    )(c8, w, b)


def _inproj_kernel(x_ref, sh_ref, sc_ref, g_ref, w_ref, cos_ref, sa_ref, sb_ref,
                   u_ref, q_ref, k_ref, v_ref, gt_ref):
    h = _rms_modulate(x_ref[...], g_ref[...], sh_ref[0], sc_ref[0]).astype(BF16)

    def proj(off, width):
        return jnp.dot(h, w_ref[:, off:off + width], preferred_element_type=F32)

    u_ref[...] = proj(0, POOL_WIDTH)
    cos, sa, sb = cos_ref[...], sa_ref[...], sb_ref[...]

    def rope(t):
        return t * cos + pltpu.roll(t, LANES - ROT_HALF, 1) * sa + pltpu.roll(t, ROT_HALF, 1) * sb

    for hd in range(ATTN_HEADS):
        sl = slice(hd * HEAD_PAD, (hd + 1) * HEAD_PAD)
        q = rope(proj(Q_OFF + hd * HEAD_PAD, HEAD_PAD))
        q_ref[:, sl] = (q * (HEAD_DIM ** -0.5)).astype(BF16)
        k_ref[:, sl] = rope(proj(K_OFF + hd * HEAD_PAD, HEAD_PAD)).astype(BF16)
    v_ref[...] = proj(V_OFF, ATTN_PAD).astype(BF16)
    gt_ref[...] = 1.0 / (1.0 + jnp.exp(-proj(G_OFF, 2 * D_MODEL)))


def _inproj(x2, shift, scale, g, w_p, cos_t, sa_t, sb_t, *, seq, tm):
    t = x2.shape[0]
    tpb = seq // tm
    row = lambda i: (i, 0)
    per_batch = lambda i: (i // tpb, 0, 0)
    per_pos = lambda i: (i % tpb, 0)
    return pl.pallas_call(
        _inproj_kernel,
        grid=(t // tm,),
        in_specs=[pl.BlockSpec((tm, D_MODEL), row),
                  pl.BlockSpec((1, 1, D_MODEL), per_batch),
                  pl.BlockSpec((1, 1, D_MODEL), per_batch),
                  _const_spec((1, D_MODEL)),
                  _const_spec((D_MODEL, IN_PROJ_PAD)),
                  pl.BlockSpec((tm, LANES), per_pos),
                  pl.BlockSpec((tm, LANES), per_pos),
                  pl.BlockSpec((tm, LANES), per_pos)],
        out_specs=[pl.BlockSpec((tm, POOL_WIDTH), row),
                   pl.BlockSpec((tm, ATTN_PAD), row),
                   pl.BlockSpec((tm, ATTN_PAD), row),
                   pl.BlockSpec((tm, ATTN_PAD), row),
                   pl.BlockSpec((tm, 2 * D_MODEL), row)],
        out_shape=[jax.ShapeDtypeStruct((t, POOL_WIDTH), F32),
                   jax.ShapeDtypeStruct((t, ATTN_PAD), BF16),
                   jax.ShapeDtypeStruct((t, ATTN_PAD), BF16),
                   jax.ShapeDtypeStruct((t, ATTN_PAD), BF16),
                   jax.ShapeDtypeStruct((t, 2 * D_MODEL), F32)],
        compiler_params=_params("arbitrary"),
        name="in_proj",
    )(x2, shift, scale, g, w_p, cos_t, sa_t, sb_t)


def _moba_kernel(q_ref, k_ref, v_ref, bsel_ref, o_ref, qa_scr, ka_scr, va_scr, *, n_blocks,
                 group):
    qi = pl.program_id(2)
    blk = MOBA_BLOCK
    lane = lax.broadcasted_iota(jnp.int32, (blk, LANES), 1)
    lane_f = lane.astype(F32)

    @pl.when(qi == 0)
    def _prepare():
        kms = jnp.dot(bsel_ref[...], k_ref[...], preferred_element_type=F32)

        def per_block(nb, carry):
            rows = pl.ds(pl.multiple_of(nb * blk, blk), blk)
            q = q_ref[rows, :]
            gate = lax.dot_general(q.astype(F32), kms, (((1,), (1,)), ((), ())),
                                   precision=HIGHEST, preferred_element_type=F32)
            valid = (lane >= BIAS_LANE0) & (lane < BIAS_LANE0 + nb)
            work = jnp.where(valid, gate, -jnp.inf)
            chosen = jnp.zeros((blk, LANES), F32)
            for _ in range(MOBA_TOPK):
                m = jnp.max(work, axis=1, keepdims=True)
                idx = jnp.min(jnp.where(work == m, lane_f, 2.0 * LANES), axis=1, keepdims=True)
                hit = (lane_f == idx) & valid
                chosen = jnp.where(hit, 1.0, chosen)
                work = jnp.where(hit, -jnp.inf, work)
            bias_lanes = (lane >= BIAS_LANE0) & (lane < BIAS_LANE0 + n_blocks)
            masked = bias_lanes & (chosen == 0.0)
            qa_scr[rows, :] = jnp.where(masked, -MASK_BIG, q.astype(F32)).astype(BF16)
            onehot = jnp.where(lane == BIAS_LANE0 + nb, 1.0, 0.0)
            ka_scr[rows, :] = jnp.where(lane < HEAD_DIM, k_ref[rows, :].astype(F32),
                                        onehot).astype(BF16)
            va_scr[rows, :] = jnp.where(lane == ONES_LANE, 1.0,
                                        v_ref[rows, :].astype(F32)).astype(BF16)
            return carry

        lax.fori_loop(0, n_blocks, per_block, 0)

    nt = (((1,), (1,)), ((), ()))
    own = pl.ds(pl.multiple_of(qi * blk, blk), blk)
    qa = qa_scr[own, :]
    q_own = jnp.where(lane < HEAD_DIM, qa, jnp.zeros_like(qa))
    s = lax.dot_general(q_own, ka_scr[own, :], nt, preferred_element_type=F32)
    r_i = lax.broadcasted_iota(jnp.int32, (blk, blk), 0)
    c_i = lax.broadcasted_iota(jnp.int32, (blk, blk), 1)
    s = jnp.where(c_i <= r_i, s, CAUSAL_NEG)
    m0 = jnp.max(s, axis=1, keepdims=True)
    p = jnp.exp(s - m0)
    acc0 = jnp.dot(p.astype(BF16), va_scr[own, :], preferred_element_type=F32)

    span = group * blk

    def past(j, carry):
        m, acc = carry
        rows = pl.ds(pl.multiple_of(j * span, span), span)
        sj = lax.dot_general(qa, ka_scr[rows, :], nt, preferred_element_type=F32)
        m_new = jnp.maximum(m, jnp.max(sj, axis=1, keepdims=True))
        pj = jnp.exp(sj - m_new)
        acc = acc * jnp.exp(m - m_new) + jnp.dot(pj.astype(BF16), va_scr[rows, :],
                                                 preferred_element_type=F32)
        return m_new, acc

    _, acc = lax.fori_loop(0, (qi + group - 1) // group, past, (m0, acc0))
    o_ref[...] = (acc / acc[:, ONES_LANE:ONES_LANE + 1]).astype(BF16)


def _moba(qp, kp, vp, bsel, *, batch, seq, group):
    n_blocks = seq // MOBA_BLOCK
    assert n_blocks % group == 0
    head = lambda b, h, i: (b, h)
    return pl.pallas_call(
        functools.partial(_moba_kernel, n_blocks=n_blocks, group=group),
        grid=(batch, ATTN_HEADS, n_blocks),
        in_specs=[pl.BlockSpec((seq, HEAD_PAD), head),
                  pl.BlockSpec((seq, HEAD_PAD), head),
                  pl.BlockSpec((seq, HEAD_PAD), head),
                  _const_spec((LANES, seq))],
        out_specs=pl.BlockSpec((MOBA_BLOCK, HEAD_PAD), lambda b, h, i: (b * n_blocks + i, h)),
        out_shape=jax.ShapeDtypeStruct((batch * seq, ATTN_PAD), BF16),
        scratch_shapes=[pltpu.VMEM((seq, HEAD_PAD), BF16)] * 3,
        compiler_params=_params("arbitrary", "arbitrary", "arbitrary"),
        name="moba_attn",
    )(qp, kp, vp, bsel)


def _merge_kernel(u_ref, uprev_ref, attn_ref, gt_ref, x_ref, poolw_ref, pscale_ref, wp_ref,
                  wa_ref, wo_ref, gate1_ref, g2_ref, sh2_ref, sc2_ref, x1_ref, h2t_ref,
                  *, tm, tpb):
    i = pl.program_id(0)
    first = (i % tpb) == 0
    pos = lax.broadcasted_iota(jnp.int32, (tm, LANES), 0) + (i % tpb) * tm
    mixed = []
    for g, w in enumerate(POOL_WINDOWS):
        sl = slice(g * POOL_GROUP_DIM, (g + 1) * POOL_GROUP_DIM)
        u = u_ref[:, sl]
        halo = jnp.where(first, 0.0, uprev_ref[:, sl])
        ext = jnp.concatenate([halo, u], axis=0)
        span = 1
        while span < w:
            ext = ext + pltpu.roll(ext, span, 0)
            span *= 2
        cnt = jnp.minimum(pos + 1, w).astype(F32)
        diff = ext[POOL_HALO:, :] / cnt - u
        m = jnp.dot(diff.astype(BF16), poolw_ref[g], preferred_element_type=F32)
        mixed.append((m * pscale_ref[:, sl]).astype(BF16))
    pooled = jnp.concatenate(mixed, axis=1)
    bp = jnp.dot(pooled, wp_ref[...], preferred_element_type=F32)
    ba = jnp.dot(attn_ref[...], wa_ref[...], preferred_element_type=F32)
    merged = gt_ref[:, :D_MODEL] * bp + gt_ref[:, D_MODEL:] * ba
    y = jnp.dot(merged.astype(BF16), wo_ref[...], preferred_element_type=F32)
    x1 = x_ref[...] + gate1_ref[0] * y
    x1_ref[...] = x1
    h2 = _rms_modulate(x1, g2_ref[...], sh2_ref[0], sc2_ref[0])
    h2t_ref[...] = h2.T.astype(BF16)


def _merge(u_pool, attn, gates, x2, poolw, pscale, wp, wa, wo, gate1, g2, shift2, scale2,
           *, seq, tm):
    t = x2.shape[0]
    tpb = seq // tm
    row = lambda i: (i, 0)
    per_batch = lambda i: (i // tpb, 0, 0)
    halo_blocks = tm // POOL_HALO
    return pl.pallas_call(
        functools.partial(_merge_kernel, tm=tm, tpb=tpb),
        grid=(t // tm,),
        in_specs=[pl.BlockSpec((tm, POOL_WIDTH), row),
                  pl.BlockSpec((POOL_HALO, POOL_WIDTH),
                               lambda i: (jnp.maximum(i * halo_blocks - 1, 0), 0)),
                  pl.BlockSpec((tm, ATTN_PAD), row),
                  pl.BlockSpec((tm, 2 * D_MODEL), row),
                  pl.BlockSpec((tm, D_MODEL), row),
                  _const_spec((len(POOL_WINDOWS), POOL_GROUP_DIM, POOL_GROUP_DIM)),
                  _const_spec((1, POOL_WIDTH)),
                  _const_spec((POOL_WIDTH, D_MODEL)),
                  _const_spec((ATTN_PAD, D_MODEL)),
                  _const_spec((D_MODEL, D_MODEL)),
                  pl.BlockSpec((1, 1, D_MODEL), per_batch),
                  _const_spec((1, D_MODEL)),
                  pl.BlockSpec((1, 1, D_MODEL), per_batch),
                  pl.BlockSpec((1, 1, D_MODEL), per_batch)],
        out_specs=[pl.BlockSpec((tm, D_MODEL), row),
                   pl.BlockSpec((D_MODEL, tm), lambda i: (0, i))],
        out_shape=[jax.ShapeDtypeStruct((t, D_MODEL), F32),
                   jax.ShapeDtypeStruct((D_MODEL, t), BF16)],
        compiler_params=_params("arbitrary"),
        name="mixer_merge",
    )(u_pool, u_pool, attn, gates, x2, poolw, pscale, wp, wa, wo, gate1, g2, shift2, scale2)


def _extract_top16(streams):
    width = streams[0][0].shape[1]
    slot = lax.broadcasted_iota(jnp.int32, (PEER_TOPK, width), 0)

    def body(r, carry):
        out = []
        for (work, rank, vals), (_, index_f) in zip(carry, streams):
            m = jnp.max(work, axis=0, keepdims=True)
            idx = jnp.min(jnp.where(work == m, index_f, jnp.inf), axis=0, keepdims=True)
            hit = index_f == idx
            out.append((jnp.where(hit, -jnp.inf, work),
                        jnp.where(hit, r.astype(F32), rank),
                        jnp.where(slot == r, m, vals)))
        return tuple(out)

    init = tuple((sc, jnp.full(sc.shape, NOT_MEMBER, F32), jnp.zeros((PEER_TOPK, width), F32))
                 for sc, _ in streams)
    return [(vals, rank) for _, rank, vals in lax.fori_loop(0, PEER_TOPK, body, init)]


def _route_kernel(h2t_ref, wqt_ref, keys_ref, r2_ref, e1_ref, n_ref, c_ref, qt_scr, *, tm):
    qt_scr[...] = jnp.dot(wqt_ref[...], h2t_ref[...], preferred_element_type=F32)
    half = PEER_TOPK // 2

    def iota_f(rows):
        return lax.broadcasted_iota(jnp.int32, (rows, tm), 0).astype(F32)

    key_f = iota_f(PEER_NKEYS)
    pair_f = jnp.concatenate([iota_f(PEER_TOPK)]
                             + [iota_f(half) + float(r * PEER_TOPK) for r in range(1, half)]
                             + [(iota_f(half) + float(half)) * float(PEER_TOPK)], axis=0)
    tail = PEER_TOPK + (half - 1) * half
    slot = lax.broadcasted_iota(jnp.int32, (half, tm), 0)

    def per_head(h, carry):
        def half_scores(p):
            rows = pl.ds(pl.multiple_of((2 * h + p) * PEER_HALF, PEER_HALF), PEER_HALF)
            return jnp.dot(keys_ref[2 * h + p], qt_scr[rows, :], precision=HIGHEST,
                           preferred_element_type=F32)

        s0 = half_scores(0)
        s1 = half_scores(1)
        (a, rank1), (b, rank2) = _extract_top16([(s0, key_f), (s1, key_f)])
        cand = jnp.concatenate([a[0:1, :] + b]
                               + [a[r:r + 1, :] + b[:half, :] for r in range(1, half)]
                               + [a[half:, :] + b[0:1, :]], axis=0)
        ((_, pick),) = _extract_top16([(cand, pair_f)])
        chosen = jnp.where(pick < NOT_MEMBER, 1.0, 0.0)
        e0 = jnp.exp(a - a[0:1, :])
        e1 = jnp.exp(b - b[0:1, :])
        n_lo = jnp.zeros((half, tm), F32)
        pref_lo = jnp.zeros((half, tm), F32)
        for r in range(half):
            lo = 0 if r == 0 else PEER_TOPK + (r - 1) * half
            cnt = PEER_TOPK if r == 0 else half
            grp = chosen[lo:lo + cnt, :]
            n_lo = jnp.where(slot == r, jnp.sum(grp, axis=0, keepdims=True), n_lo)
            pref_lo = jnp.where(slot == r, jnp.sum(grp * e1[:cnt, :], axis=0, keepdims=True),
                                pref_lo)
        n = jnp.concatenate([n_lo, chosen[tail:, :]], axis=0)
        pref = jnp.concatenate([pref_lo, chosen[tail:, :]], axis=0)
        z = jnp.sum(e0 * pref, axis=0, keepdims=True)
        n_dense = jnp.zeros((PEER_NKEYS, tm), F32)
        for r in range(PEER_TOPK):
            n_dense = jnp.where(rank1 == float(r), n[r:r + 1, :], n_dense)
        r2_ref[h] = rank2
        e1_ref[h] = jnp.exp(s1 - b[0:1, :])
        n_ref[h] = n_dense
        c_ref[h] = jnp.exp(s0 - a[0:1, :]) / z
        return carry

    lax.fori_loop(0, PEER_HEADS, per_head, 0)


def _route(h2t, wqt, keys, *, tm):
    t = h2t.shape[1]
    dense = jax.ShapeDtypeStruct((PEER_HEADS, PEER_NKEYS, t), F32)
    dense_spec = pl.BlockSpec((PEER_HEADS, PEER_NKEYS, tm), lambda i: (0, 0, i))
    return pl.pallas_call(
        functools.partial(_route_kernel, tm=tm),
        grid=(t // tm,),
        in_specs=[pl.BlockSpec((D_MODEL, tm), lambda i: (0, i)),
                  _const_spec((2 * PEER_HEADS * PEER_HALF, D_MODEL)),
                  _const_spec((2 * PEER_HEADS, PEER_NKEYS, PEER_HALF))],
        out_specs=[dense_spec] * 4,
        out_shape=[dense] * 4,
        scratch_shapes=[pltpu.VMEM((2 * PEER_HEADS * PEER_HALF, tm), F32)],
        compiler_params=_params("arbitrary"),
        name="peer_route",
    )(h2t, wqt, keys)


def _experts_kernel(h2t_ref, u_ref, vt_ref, r2_ref, e1_ref, n_ref, c_ref, x1_ref, gate2_ref,
                    gf_ref, o_ref, acc_scr, p_scr, *, tm, sub_blocks):
    e = pl.program_id(1)

    @pl.when(e == 0)
    def _zero():
        acc_scr[...] = jnp.zeros_like(acc_scr)

    ht = h2t_ref[...]
    for jb in range(sub_blocks):
        rows = slice(jb * PEER_NKEYS, (jb + 1) * PEER_NKEYS)
        act = jnp.dot(u_ref[rows, :], ht, preferred_element_type=F32)
        for cc in range(tm // LANES):
            cols = slice(cc * LANES, (cc + 1) * LANES)
            a = act[:, cols]
            gelu = 0.5 * a * (1.0 + lax.erf(a * SQRT_HALF))
            w = jnp.zeros((PEER_NKEYS, LANES), F32)
            for h in range(PEER_HEADS):
                keep = r2_ref[h, :, cols] < n_ref[h, jb:jb + 1, cols]
                w = w + jnp.where(keep, e1_ref[h, :, cols] * c_ref[h, jb:jb + 1, cols], 0.0)
            p_scr[rows, cols] = (w * gelu).astype(BF16)
    acc_scr[...] += jnp.dot(vt_ref[...], p_scr[...], preferred_element_type=F32)

    @pl.when(e == pl.num_programs(1) - 1)
    def _finish():
        x2 = x1_ref[...] + gate2_ref[0] * acc_scr[...].T
        ms = jnp.mean(x2 * x2, axis=-1, keepdims=True)
        o_ref[...] = x2 * lax.rsqrt(ms + EPS) * gf_ref[...]


def _experts(h2t, u_b, vt_b, r2, e1, n, c, x1, gate2, gf, *, seq, tm, sub_blocks):
    t = h2t.shape[1]
    tpb = seq // tm
    eb = sub_blocks * PEER_NKEYS
    tok3 = lambda i, e: (0, 0, i)
    return pl.pallas_call(
        functools.partial(_experts_kernel, tm=tm, sub_blocks=sub_blocks),
        grid=(t // tm, PEER_EXPERTS // eb),
        in_specs=[pl.BlockSpec((D_MODEL, tm), lambda i, e: (0, i)),
                  pl.BlockSpec((eb, D_MODEL), lambda i, e: (e, 0)),
                  pl.BlockSpec((D_MODEL, eb), lambda i, e: (0, e)),
                  pl.BlockSpec((PEER_HEADS, PEER_NKEYS, tm), tok3),
                  pl.BlockSpec((PEER_HEADS, PEER_NKEYS, tm), tok3),
                  pl.BlockSpec((PEER_HEADS, sub_blocks, tm), lambda i, e: (0, e, i)),
                  pl.BlockSpec((PEER_HEADS, sub_blocks, tm), lambda i, e: (0, e, i)),
                  pl.BlockSpec((tm, D_MODEL), lambda i, e: (i, 0)),
                  pl.BlockSpec((1, 1, D_MODEL), lambda i, e: (i // tpb, 0, 0)),
                  pl.BlockSpec((1, D_MODEL), lambda i, e: (0, 0))],
        out_specs=pl.BlockSpec((tm, D_MODEL), lambda i, e: (i, 0)),
        out_shape=jax.ShapeDtypeStruct((t, D_MODEL), F32),
        scratch_shapes=[pltpu.VMEM((D_MODEL, tm), F32), pltpu.VMEM((eb, tm), BF16)],
        compiler_params=_params("arbitrary", "arbitrary"),
        name="peer_experts",
    )(h2t, u_b, vt_b, r2, e1, n, c, x1, gate2, gf)


def _pad_heads(w):
    d = w.shape[0]
    w = w.reshape(d, ATTN_HEADS, HEAD_DIM)
    return jnp.pad(w, ((0, 0), (0, 0), (0, HEAD_PAD - HEAD_DIM))).reshape(d, ATTN_PAD)


def _rope_tables(seq):
    inv = ROPE_THETA ** (-jnp.arange(ROT_HALF, dtype=F32) / ROT_HALF)
    ang = jnp.arange(seq).astype(F32)[:, None] * inv[None, :]
    cos, sin = jnp.cos(ang), jnp.sin(ang)
    z = lambda n: jnp.zeros((seq, n), F32)
    cos_t = jnp.concatenate([cos, cos, jnp.ones((seq, LANES - 2 * ROT_HALF), F32)], axis=1)
    sa_t = jnp.concatenate([-sin, z(LANES - ROT_HALF)], axis=1)
    sb_t = jnp.concatenate([z(ROT_HALF), sin, z(LANES - 2 * ROT_HALF)], axis=1)
    return cos_t, sa_t, sb_t


def _block_mean_rows(seq):
    r = jnp.arange(LANES)[:, None]
    s = jnp.arange(seq)[None, :]
    return jnp.where(r - BIAS_LANE0 == s // MOBA_BLOCK, 1.0 / MOBA_BLOCK, 0.0).astype(BF16)


def kernel(x, c, w_ada, b_ada, norm_mix_g, w_in, pool_w, pool_scale, w_branch_pool,
           w_branch_attn, w_out, norm_ffn_g, peer_wq, peer_sub_keys, peer_u, peer_v,
           norm_final_g):
    batch, seq, d = x.shape
    depth = w_ada.shape[0]
    assert d == D_MODEL and batch <= 8 and seq % 512 == 0
    assert seq // MOBA_BLOCK <= LANES - BIAS_LANE0
    assert depth == 1
    t = batch * seq
    x2 = x.reshape(t, d)
    c8 = jnp.pad(c, ((0, 8 - batch), (0, 0)))
    cos_t, sa_t, sb_t = _rope_tables(seq)
    bsel = _block_mean_rows(seq)
    row1 = lambda v: v.reshape(1, -1)
    per_batch = lambda v: v.reshape(batch, 1, d)

    for l in range(depth):
        mod = _ada(c8, w_ada[l], row1(b_ada[l]))[:batch]
        shift1, scale1, gate1, shift2, scale2, gate2 = [per_batch(m) for m in
                                                        jnp.split(mod, 6, axis=-1)]
        wl = w_in[l]
        w_p = jnp.concatenate(
            [wl[:, :POOL_WIDTH],
             _pad_heads(wl[:, POOL_WIDTH:POOL_WIDTH + ATTN_WIDTH]),
             _pad_heads(wl[:, POOL_WIDTH + ATTN_WIDTH:POOL_WIDTH + 2 * ATTN_WIDTH]),
             _pad_heads(wl[:, POOL_WIDTH + 2 * ATTN_WIDTH:POOL_WIDTH + 3 * ATTN_WIDTH]),
             wl[:, POOL_WIDTH + 3 * ATTN_WIDTH:]], axis=1).astype(BF16)
        u_pool, qp, kp, vp, gates = _inproj(x2, shift1, scale1, row1(norm_mix_g[l]), w_p,
                                            cos_t, sa_t, sb_t, seq=seq, tm=256)
        attn = _moba(qp, kp, vp, bsel, batch=batch, seq=seq, group=4)
        wa_p = _pad_heads(w_branch_attn[l].T).T.astype(BF16)
        x1, h2t = _merge(u_pool, attn, gates, x2, pool_w[l].astype(BF16), row1(pool_scale[l]),
                         w_branch_pool[l].astype(BF16), wa_p, w_out[l].astype(BF16), gate1,
                         row1(norm_ffn_g[l]), shift2, scale2, seq=seq, tm=256)
        keys = peer_sub_keys[l].reshape(2 * PEER_HEADS, PEER_NKEYS, PEER_HALF)
        r2, e1, n, cden = _route(h2t, peer_wq[l].T.astype(BF16), keys, tm=256)
        x2 = _experts(h2t, peer_u[l].astype(BF16), peer_v[l].T.astype(BF16), r2, e1, n, cden,
                      x1, gate2, row1(norm_final_g), seq=seq, tm=512, sub_blocks=8)
    return x2.reshape(batch, seq, d)
```

```python
import functools
import math

import jax
import jax.numpy as jnp
import numpy as np
from jax import lax
from jax.experimental import pallas as pl
from jax.experimental.pallas import tpu as pltpu

F32 = jnp.float32
BF16 = jnp.bfloat16
HIGHEST = lax.Precision.HIGHEST

D_MODEL = 1024
POOL_WINDOWS = (2, 4, 8, 16)
POOL_WIDTH = 512
POOL_GROUP_DIM = 128
POOL_HALO = 16
ATTN_HEADS = 8
HEAD_DIM = 64
ATTN_WIDTH = 512
MOBA_BLOCK = 256
MOBA_TOPK = 3
ROPE_THETA = 500000.0
ROT_HALF = 8
PEER_HEADS = 8
PEER_NKEYS = 128
PEER_EXPERTS = PEER_NKEYS * PEER_NKEYS
PEER_HALF = 128
PEER_TOPK = 16
EPS = 1e-6

LANES = 128
SUBLANES = 8
HEAD_PAD = LANES
ATTN_PAD = ATTN_HEADS * HEAD_PAD
BIAS_LANE0 = HEAD_DIM
ONES_LANE = HEAD_DIM
MASK_BIG = 2.0 ** 100
CAUSAL_NEG = -1e30
NOT_MEMBER = 255.0
SQRT_HALF = float(np.sqrt(0.5).astype(np.float32))
VMEM_LIMIT = 56 * 1024 * 1024

IN_PROJ_PAD = POOL_WIDTH + 3 * ATTN_PAD + 2 * D_MODEL
Q_OFF = POOL_WIDTH
K_OFF = Q_OFF + ATTN_PAD
V_OFF = K_OFF + ATTN_PAD
G_OFF = V_OFF + ATTN_PAD


def _params(*semantics):
    return pltpu.CompilerParams(dimension_semantics=semantics, vmem_limit_bytes=VMEM_LIMIT)


def _const_spec(shape):
    nd = len(shape)
    return pl.BlockSpec(shape, lambda *_: (0,) * nd, pipeline_mode=pl.Buffered(1))


def _rms_modulate(x, g, shift, scale):
    ms = jnp.mean(x * x, axis=-1, keepdims=True)
    return (x * lax.rsqrt(ms + EPS) * g) * (1.0 + scale) + shift


def _ada_kernel(c_ref, w_ref, b_ref, o_ref):
    c = c_ref[...]
    act = c / (1.0 + jnp.exp(-c))
    o_ref[...] = jnp.dot(act, w_ref[...], precision=HIGHEST,
                         preferred_element_type=F32) + b_ref[...]


def _ada(c8, w, b):
    n = w.shape[1]
    tn = 1536
    return pl.pallas_call(
        _ada_kernel,
        grid=(n // tn,),
        in_specs=[pl.BlockSpec((8, D_MODEL), lambda j: (0, 0)),
                  pl.BlockSpec((D_MODEL, tn), lambda j: (0, j)),
                  pl.BlockSpec((1, tn), lambda j: (0, j))],
        out_specs=pl.BlockSpec((8, tn), lambda j: (0, j)),
        out_shape=jax.ShapeDtypeStruct((8, n), F32),
        compiler_params=_params("arbitrary"),
        name="ada_mod",
    )(c8, w, b)


def _inproj_kernel(x_ref, sh_ref, sc_ref, g_ref, w_ref, cos_ref, sa_ref, sb_ref,
                   u_ref, q_ref, k_ref, v_ref, gt_ref):
    h = _rms_modulate(x_ref[...], g_ref[...], sh_ref[0], sc_ref[0]).astype(BF16)

    def proj(off, width):
        return jnp.dot(h, w_ref[:, off:off + width], preferred_element_type=F32)

    u_ref[...] = proj(0, POOL_WIDTH)
    cos, sa, sb = cos_ref[...], sa_ref[...], sb_ref[...]

    def rope(t):
        return t * cos + pltpu.roll(t, LANES - ROT_HALF, 1) * sa + pltpu.roll(t, ROT_HALF, 1) * sb

    for hd in range(ATTN_HEADS):
        sl = slice(hd * HEAD_PAD, (hd + 1) * HEAD_PAD)
        q = rope(proj(Q_OFF + hd * HEAD_PAD, HEAD_PAD))
        q_ref[:, sl] = (q * (HEAD_DIM ** -0.5)).astype(BF16)
        k_ref[:, sl] = rope(proj(K_OFF + hd * HEAD_PAD, HEAD_PAD)).astype(BF16)
    v_ref[...] = proj(V_OFF, ATTN_PAD).astype(BF16)
    gt_ref[...] = 1.0 / (1.0 + jnp.exp(-proj(G_OFF, 2 * D_MODEL)))


def _inproj(x2, shift, scale, g, w_p, cos_t, sa_t, sb_t, *, seq, tm):
    t = x2.shape[0]
    tpb = seq // tm
    row = lambda i: (i, 0)
    per_batch = lambda i: (i // tpb, 0, 0)
    per_pos = lambda i: (i % tpb, 0)
    return pl.pallas_call(
        _inproj_kernel,
        grid=(t // tm,),
        in_specs=[pl.BlockSpec((tm, D_MODEL), row),
                  pl.BlockSpec((1, 1, D_MODEL), per_batch),
                  pl.BlockSpec((1, 1, D_MODEL), per_batch),
                  _const_spec((1, D_MODEL)),
                  _const_spec((D_MODEL, IN_PROJ_PAD)),
                  pl.BlockSpec((tm, LANES), per_pos),
                  pl.BlockSpec((tm, LANES), per_pos),
                  pl.BlockSpec((tm, LANES), per_pos)],
        out_specs=[pl.BlockSpec((tm, POOL_WIDTH), row),
                   pl.BlockSpec((tm, ATTN_PAD), row),
                   pl.BlockSpec((tm, ATTN_PAD), row),
                   pl.BlockSpec((tm, ATTN_PAD), row),
                   pl.BlockSpec((tm, 2 * D_MODEL), row)],
        out_shape=[jax.ShapeDtypeStruct((t, POOL_WIDTH), F32),
                   jax.ShapeDtypeStruct((t, ATTN_PAD), BF16),
                   jax.ShapeDtypeStruct((t, ATTN_PAD), BF16),
                   jax.ShapeDtypeStruct((t, ATTN_PAD), BF16),
                   jax.ShapeDtypeStruct((t, 2 * D_MODEL), F32)],
        compiler_params=_params("arbitrary"),
        name="in_proj",
    )(x2, shift, scale, g, w_p, cos_t, sa_t, sb_t)


def _moba_kernel(q_ref, k_ref, v_ref, bsel_ref, o_ref, qa_scr, ka_scr, va_scr, *, n_blocks,
                 group):
    qi = pl.program_id(2)
    blk = MOBA_BLOCK
    lane = lax.broadcasted_iota(jnp.int32, (blk, LANES), 1)
    lane_f = lane.astype(F32)

    @pl.when(qi == 0)
    def _prepare():
        kms = jnp.dot(bsel_ref[...], k_ref[...], preferred_element_type=F32)

        def per_block(nb, carry):
            rows = pl.ds(pl.multiple_of(nb * blk, blk), blk)
            q = q_ref[rows, :]
            gate = lax.dot_general(q.astype(F32), kms, (((1,), (1,)), ((), ())),
                                   precision=HIGHEST, preferred_element_type=F32)
            valid = (lane >= BIAS_LANE0) & (lane < BIAS_LANE0 + nb)
            work = jnp.where(valid, gate, -jnp.inf)
            chosen = jnp.zeros((blk, LANES), F32)
            for _ in range(MOBA_TOPK):
                m = jnp.max(work, axis=1, keepdims=True)
                idx = jnp.min(jnp.where(work == m, lane_f, 2.0 * LANES), axis=1, keepdims=True)
                hit = (lane_f == idx) & valid
                chosen = jnp.where(hit, 1.0, chosen)
                work = jnp.where(hit, -jnp.inf, work)
            bias_lanes = (lane >= BIAS_LANE0) & (lane < BIAS_LANE0 + n_blocks)
            masked = bias_lanes & (chosen == 0.0)
            qa_scr[rows, :] = jnp.where(masked, -MASK_BIG, q.astype(F32)).astype(BF16)
            onehot = jnp.where(lane == BIAS_LANE0 + nb, 1.0, 0.0)
            ka_scr[rows, :] = jnp.where(lane < HEAD_DIM, k_ref[rows, :].astype(F32),
                                        onehot).astype(BF16)
            va_scr[rows, :] = jnp.where(lane == ONES_LANE, 1.0,
                                        v_ref[rows, :].astype(F32)).astype(BF16)
            return carry

        lax.fori_loop(0, n_blocks, per_block, 0)

    nt = (((1,), (1,)), ((), ()))
    own = pl.ds(pl.multiple_of(qi * blk, blk), blk)
    qa = qa_scr[own, :]
    q_own = jnp.where(lane < HEAD_DIM, qa, jnp.zeros_like(qa))
    s = lax.dot_general(q_own, ka_scr[own, :], nt, preferred_element_type=F32)
    r_i = lax.broadcasted_iota(jnp.int32, (blk, blk), 0)
    c_i = lax.broadcasted_iota(jnp.int32, (blk, blk), 1)
    s = jnp.where(c_i <= r_i, s, CAUSAL_NEG)
    m0 = jnp.max(s, axis=1, keepdims=True)
    p = jnp.exp(s - m0)
    acc0 = jnp.dot(p.astype(BF16), va_scr[own, :], preferred_element_type=F32)

    span = group * blk

    def past(j, carry):
        m, acc = carry
        rows = pl.ds(pl.multiple_of(j * span, span), span)
        sj = lax.dot_general(qa, ka_scr[rows, :], nt, preferred_element_type=F32)
        m_new = jnp.maximum(m, jnp.max(sj, axis=1, keepdims=True))
        pj = jnp.exp(sj - m_new)
        acc = acc * jnp.exp(m - m_new) + jnp.dot(pj.astype(BF16), va_scr[rows, :],
                                                 preferred_element_type=F32)
        return m_new, acc

    _, acc = lax.fori_loop(0, (qi + group - 1) // group, past, (m0, acc0))
    o_ref[...] = (acc / acc[:, ONES_LANE:ONES_LANE + 1]).astype(BF16)


def _moba(qp, kp, vp, bsel, *, batch, seq, group):
    n_blocks = seq // MOBA_BLOCK
    assert n_blocks % group == 0
    head = lambda b, h, i: (b, h)
    return pl.pallas_call(
        functools.partial(_moba_kernel, n_blocks=n_blocks, group=group),
        grid=(batch, ATTN_HEADS, n_blocks),
        in_specs=[pl.BlockSpec((seq, HEAD_PAD), head),
                  pl.BlockSpec((seq, HEAD_PAD), head),
                  pl.BlockSpec((seq, HEAD_PAD), head),
                  _const_spec((LANES, seq))],
        out_specs=pl.BlockSpec((MOBA_BLOCK, HEAD_PAD), lambda b, h, i: (b * n_blocks + i, h)),
        out_shape=jax.ShapeDtypeStruct((batch * seq, ATTN_PAD), BF16),
        scratch_shapes=[pltpu.VMEM((seq, HEAD_PAD), BF16)] * 3,
        compiler_params=_params("arbitrary", "arbitrary", "arbitrary"),
        name="moba_attn",
    )(qp, kp, vp, bsel)


def _merge_kernel(u_ref, uprev_ref, attn_ref, gt_ref, x_ref, poolw_ref, pscale_ref, wp_ref,
                  wa_ref, wo_ref, gate1_ref, g2_ref, sh2_ref, sc2_ref, x1_ref, h2t_ref,
                  *, tm, tpb):
    i = pl.program_id(0)
    first = (i % tpb) == 0
    pos = lax.broadcasted_iota(jnp.int32, (tm, LANES), 0) + (i % tpb) * tm
    mixed = []
    for g, w in enumerate(POOL_WINDOWS):
        sl = slice(g * POOL_GROUP_DIM, (g + 1) * POOL_GROUP_DIM)
        u = u_ref[:, sl]
        halo = jnp.where(first, 0.0, uprev_ref[:, sl])
        ext = jnp.concatenate([halo, u], axis=0)
        span = 1
        while span < w:
            ext = ext + pltpu.roll(ext, span, 0)
            span *= 2
        cnt = jnp.minimum(pos + 1, w).astype(F32)
        diff = ext[POOL_HALO:, :] / cnt - u
        m = jnp.dot(diff.astype(BF16), poolw_ref[g], preferred_element_type=F32)
        mixed.append((m * pscale_ref[:, sl]).astype(BF16))
    pooled = jnp.concatenate(mixed, axis=1)
    bp = jnp.dot(pooled, wp_ref[...], preferred_element_type=F32)
    ba = jnp.dot(attn_ref[...], wa_ref[...], preferred_element_type=F32)
    merged = gt_ref[:, :D_MODEL] * bp + gt_ref[:, D_MODEL:] * ba
    y = jnp.dot(merged.astype(BF16), wo_ref[...], preferred_element_type=F32)
    x1 = x_ref[...] + gate1_ref[0] * y
    x1_ref[...] = x1
    h2 = _rms_modulate(x1, g2_ref[...], sh2_ref[0], sc2_ref[0])
    h2t_ref[...] = h2.T.astype(BF16)


def _merge(u_pool, attn, gates, x2, poolw, pscale, wp, wa, wo, gate1, g2, shift2, scale2,
           *, seq, tm):
    t = x2.shape[0]
    tpb = seq // tm
    row = lambda i: (i, 0)
    per_batch = lambda i: (i // tpb, 0, 0)
    halo_blocks = tm // POOL_HALO
    return pl.pallas_call(
        functools.partial(_merge_kernel, tm=tm, tpb=tpb),
        grid=(t // tm,),
        in_specs=[pl.BlockSpec((tm, POOL_WIDTH), row),
                  pl.BlockSpec((POOL_HALO, POOL_WIDTH),
                               lambda i: (jnp.maximum(i * halo_blocks - 1, 0), 0)),
                  pl.BlockSpec((tm, ATTN_PAD), row),
                  pl.BlockSpec((tm, 2 * D_MODEL), row),
                  pl.BlockSpec((tm, D_MODEL), row),
                  _const_spec((len(POOL_WINDOWS), POOL_GROUP_DIM, POOL_GROUP_DIM)),
                  _const_spec((1, POOL_WIDTH)),
                  _const_spec((POOL_WIDTH, D_MODEL)),
                  _const_spec((ATTN_PAD, D_MODEL)),
                  _const_spec((D_MODEL, D_MODEL)),
                  pl.BlockSpec((1, 1, D_MODEL), per_batch),
                  _const_spec((1, D_MODEL)),
                  pl.BlockSpec((1, 1, D_MODEL), per_batch),
                  pl.BlockSpec((1, 1, D_MODEL), per_batch)],
        out_specs=[pl.BlockSpec((tm, D_MODEL), row),
                   pl.BlockSpec((D_MODEL, tm), lambda i: (0, i))],
        out_shape=[jax.ShapeDtypeStruct((t, D_MODEL), F32),
                   jax.ShapeDtypeStruct((D_MODEL, t), BF16)],
        compiler_params=_params("arbitrary"),
        name="mixer_merge",
    )(u_pool, u_pool, attn, gates, x2, poolw, pscale, wp, wa, wo, gate1, g2, shift2, scale2)


def _extract_top16(streams):
    width = streams[0][0].shape[1]
    slot = lax.broadcasted_iota(jnp.int32, (PEER_TOPK, width), 0)

    def body(r, carry):
        out = []
        for (work, rank, vals), (_, index_f) in zip(carry, streams):
            m = jnp.max(work, axis=0, keepdims=True)
            idx = jnp.min(jnp.where(work == m, index_f, jnp.inf), axis=0, keepdims=True)
            hit = index_f == idx
            out.append((jnp.where(hit, -jnp.inf, work),
                        jnp.where(hit, lax.convert_element_type(r, F32), rank),
                        jnp.where(slot == r, m, vals)))
        return tuple(out)

    init = tuple((sc, jnp.full(sc.shape, NOT_MEMBER, F32), jnp.zeros((PEER_TOPK, width), F32))
                 for sc, _ in streams)
    return [(vals, rank) for _, rank, vals in lax.fori_loop(0, PEER_TOPK, body, init)]


def _route_kernel(h2t_ref, wqt_ref, keys_ref, r2_ref, e1_ref, n_ref, c_ref, qt_scr, *, tm):
    qt_scr[...] = jnp.dot(wqt_ref[...], h2t_ref[...], preferred_element_type=F32)
    half = PEER_TOPK // 2

    def iota_f(rows):
        return lax.broadcasted_iota(jnp.int32, (rows, tm), 0).astype(F32)

    key_f = iota_f(PEER_NKEYS)
    pair_f = jnp.concatenate([iota_f(PEER_TOPK)]
                             + [iota_f(half) + float(r * PEER_TOPK) for r in range(1, half)]
                             + [(iota_f(half) + float(half)) * float(PEER_TOPK)], axis=0)
    tail = PEER_TOPK + (half - 1) * half
    slot = lax.broadcasted_iota(jnp.int32, (half, tm), 0)

    def per_head(h, carry):
        def half_scores(p):
            rows = pl.ds(pl.multiple_of((2 * h + p) * PEER_HALF, PEER_HALF), PEER_HALF)
            return jnp.dot(keys_ref[2 * h + p], qt_scr[rows, :], precision=HIGHEST,
                           preferred_element_type=F32)

        s0 = half_scores(0)
        s1 = half_scores(1)
        (a, rank1), (b, rank2) = _extract_top16([(s0, key_f), (s1, key_f)])
        cand = jnp.concatenate([a[0:1, :] + b]
                               + [a[r:r + 1, :] + b[:half, :] for r in range(1, half)]
                               + [a[half:, :] + b[0:1, :]], axis=0)
        ((_, pick),) = _extract_top16([(cand, pair_f)])
        chosen = jnp.where(pick < NOT_MEMBER, 1.0, 0.0)
        e0 = jnp.exp(a - a[0:1, :])
        e1 = jnp.exp(b - b[0:1, :])
        n_lo = jnp.zeros((half, tm), F32)
        pref_lo = jnp.zeros((half, tm), F32)
        for r in range(half):
            lo = 0 if r == 0 else PEER_TOPK + (r - 1) * half
            cnt = PEER_TOPK if r == 0 else half
            grp = chosen[lo:lo + cnt, :]
            n_lo = jnp.where(slot == r, jnp.sum(grp, axis=0, keepdims=True), n_lo)
            pref_lo = jnp.where(slot == r, jnp.sum(grp * e1[:cnt, :], axis=0, keepdims=True),
                                pref_lo)
        n = jnp.concatenate([n_lo, chosen[tail:, :]], axis=0)
        pref = jnp.concatenate([pref_lo, chosen[tail:, :]], axis=0)
        z = jnp.sum(e0 * pref, axis=0, keepdims=True)
        n_dense = jnp.zeros((PEER_NKEYS, tm), F32)
        for r in range(PEER_TOPK):
            n_dense = jnp.where(rank1 == float(r), n[r:r + 1, :], n_dense)
        r2_ref[h] = rank2
        e1_ref[h] = jnp.exp(s1 - b[0:1, :])
        n_ref[h] = n_dense
        c_ref[h] = 0.5 * jnp.exp(s0 - a[0:1, :]) / z
        return carry

    lax.fori_loop(0, PEER_HEADS, per_head, 0)


def _route(h2t, wqt, keys, *, tm):
    t = h2t.shape[1]
    dense = jax.ShapeDtypeStruct((PEER_HEADS, PEER_NKEYS, t), F32)
    dense_spec = pl.BlockSpec((PEER_HEADS, PEER_NKEYS, tm), lambda i: (0, 0, i))
    return pl.pallas_call(
        functools.partial(_route_kernel, tm=tm),
        grid=(t // tm,),
        in_specs=[pl.BlockSpec((D_MODEL, tm), lambda i: (0, i)),
                  _const_spec((2 * PEER_HEADS * PEER_HALF, D_MODEL)),
                  _const_spec((2 * PEER_HEADS, PEER_NKEYS, PEER_HALF))],
        out_specs=[dense_spec] * 4,
        out_shape=[dense] * 4,
        scratch_shapes=[pltpu.VMEM((2 * PEER_HEADS * PEER_HALF, tm), F32)],
        compiler_params=_params("arbitrary"),
        name="peer_route",
    )(h2t, wqt, keys)


def _experts_kernel(h2t_ref, u_ref, vt_ref, r2_ref, e1_ref, n_ref, c_ref, x1_ref, gate2_ref,
                    gf_ref, o_ref, acc_scr, act_a, act_b, p_a, p_b, *, tm, sub_blocks):
    s = pl.program_id(1)

    @pl.when(s == 0)
    def _init():
        acc_scr[...] = jnp.zeros_like(acc_scr)
        act_b[...] = jnp.zeros_like(act_b)
        p_a[...] = jnp.zeros_like(p_a)

    quad = 4 * SUBLANES

    def step(act_new, act_old, p_new, p_old):
        n_cc = tm // LANES
        n_kq = PEER_NKEYS // quad
        n_piece = n_kq
        eb = act_new.shape[0]
        m1, k1 = eb // n_cc, D_MODEL // n_piece
        m2, k2 = D_MODEL // n_cc, eb // n_piece

        def matmul_pieces(cc, piece):
            u_rows = slice(cc * m1, (cc + 1) * m1)
            ks = slice(piece * k1, (piece + 1) * k1)
            part = jnp.dot(u_ref[u_rows, ks], h2t_ref[ks, :], preferred_element_type=F32)
            if piece == 0:
                act_new[u_rows, :] = part
            else:
                act_new[u_rows, :] += part
            d_rows = slice(cc * m2, (cc + 1) * m2)
            ks = slice(piece * k2, (piece + 1) * k2)
            acc_scr[d_rows, :] += jnp.dot(vt_ref[d_rows, ks], p_old[ks, :],
                                          preferred_element_type=F32)

        for cc in range(n_cc):
            cols = slice(cc * LANES, (cc + 1) * LANES)
            for kq in range(n_kq):
                matmul_pieces(cc, kq)
                i2 = slice(kq * quad, (kq + 1) * quad)
                w = [jnp.zeros((quad, LANES), F32) for _ in range(sub_blocks)]
                for h in range(PEER_HEADS):
                    r2 = r2_ref[h, i2, cols]
                    e1 = e1_ref[h, i2, cols]
                    for jb in range(sub_blocks):
                        keep = r2 < n_ref[h, jb:jb + 1, cols]
                        w[jb] = w[jb] + jnp.where(keep, e1 * c_ref[h, jb:jb + 1, cols], 0.0)
                for jb in range(sub_blocks):
                    rows = slice(jb * PEER_NKEYS + kq * quad, jb * PEER_NKEYS + (kq + 1) * quad)
                    a = act_old[rows, cols]
                    p_new[rows, cols] = (w[jb] * (a * (1.0 + lax.erf(a * SQRT_HALF)))).astype(BF16)

    @pl.when(s % 2 == 0)
    def _even():
        step(act_a, act_b, p_b, p_a)

    @pl.when(s % 2 == 1)
    def _odd():
        step(act_b, act_a, p_a, p_b)

    @pl.when(s == pl.num_programs(1) - 1)
    def _finish():
        x2 = x1_ref[...] + gate2_ref[0] * acc_scr[...].T
        ms = jnp.mean(x2 * x2, axis=-1, keepdims=True)
        o_ref[...] = x2 * lax.rsqrt(ms + EPS) * gf_ref[...]


def _experts(h2t, u_b, vt_b, r2, e1, n, c, x1, gate2, gf, *, seq, tm, sub_blocks):
    t = h2t.shape[1]
    tpb = seq // tm
    eb = sub_blocks * PEER_NKEYS
    n_eb = PEER_EXPERTS // eb
    tok3 = lambda i, s: (0, 0, i)
    blk = lambda s, lag: jnp.clip(s - lag, 0, n_eb - 1)
    return pl.pallas_call(
        functools.partial(_experts_kernel, tm=tm, sub_blocks=sub_blocks),
        grid=(t // tm, n_eb + 2),
        in_specs=[pl.BlockSpec((D_MODEL, tm), lambda i, s: (0, i)),
                  pl.BlockSpec((eb, D_MODEL), lambda i, s: (blk(s, 0), 0)),
                  pl.BlockSpec((D_MODEL, eb), lambda i, s: (0, blk(s, 2))),
                  pl.BlockSpec((PEER_HEADS, PEER_NKEYS, tm), tok3),
                  pl.BlockSpec((PEER_HEADS, PEER_NKEYS, tm), tok3),
                  pl.BlockSpec((PEER_HEADS, sub_blocks, tm), lambda i, s: (0, blk(s, 1), i)),
                  pl.BlockSpec((PEER_HEADS, sub_blocks, tm), lambda i, s: (0, blk(s, 1), i)),
                  pl.BlockSpec((tm, D_MODEL), lambda i, s: (i, 0)),
                  pl.BlockSpec((1, 1, D_MODEL), lambda i, s: (i // tpb, 0, 0)),
                  pl.BlockSpec((1, D_MODEL), lambda i, s: (0, 0))],
        out_specs=pl.BlockSpec((tm, D_MODEL), lambda i, s: (i, 0)),
        out_shape=jax.ShapeDtypeStruct((t, D_MODEL), F32),
        scratch_shapes=[pltpu.VMEM((D_MODEL, tm), F32),
                        pltpu.VMEM((eb, tm), F32), pltpu.VMEM((eb, tm), F32),
                        pltpu.VMEM((eb, tm), BF16), pltpu.VMEM((eb, tm), BF16)],
        compiler_params=_params("arbitrary", "arbitrary"),
        name="peer_experts",
    )(h2t, u_b, vt_b, r2, e1, n, c, x1, gate2, gf)


def _pad_heads(w):
    d = w.shape[0]
    w = w.reshape(d, ATTN_HEADS, HEAD_DIM)
    return jnp.pad(w, ((0, 0), (0, 0), (0, HEAD_PAD - HEAD_DIM))).reshape(d, ATTN_PAD)


def _rope_tables(seq):
    inv = ROPE_THETA ** (-jnp.arange(ROT_HALF, dtype=F32) / ROT_HALF)
    ang = jnp.arange(seq).astype(F32)[:, None] * inv[None, :]
    cos, sin = jnp.cos(ang), jnp.sin(ang)
    z = lambda n: jnp.zeros((seq, n), F32)
    cos_t = jnp.concatenate([cos, cos, jnp.ones((seq, LANES - 2 * ROT_HALF), F32)], axis=1)
    sa_t = jnp.concatenate([-sin, z(LANES - ROT_HALF)], axis=1)
    sb_t = jnp.concatenate([z(ROT_HALF), sin, z(LANES - 2 * ROT_HALF)], axis=1)
    return cos_t, sa_t, sb_t


def _block_mean_rows(seq):
    r = jnp.arange(LANES)[:, None]
    s = jnp.arange(seq)[None, :]
    return jnp.where(r - BIAS_LANE0 == s // MOBA_BLOCK, 1.0 / MOBA_BLOCK, 0.0).astype(BF16)


def kernel(x, c, w_ada, b_ada, norm_mix_g, w_in, pool_w, pool_scale, w_branch_pool,
           w_branch_attn, w_out, norm_ffn_g, peer_wq, peer_sub_keys, peer_u, peer_v,
           norm_final_g):
    batch, seq, d = x.shape
    depth = w_ada.shape[0]
    assert d == D_MODEL and batch <= 8 and seq % 512 == 0
    assert seq // MOBA_BLOCK <= LANES - BIAS_LANE0
    assert depth == 1
    t = batch * seq
    x2 = x.reshape(t, d)
    c8 = jnp.pad(c, ((0, 8 - batch), (0, 0)))
    cos_t, sa_t, sb_t = _rope_tables(seq)
    bsel = _block_mean_rows(seq)
    row1 = lambda v: v.reshape(1, -1)
    per_batch = lambda v: v.reshape(batch, 1, d)

    for l in range(depth):
        mod = _ada(c8, w_ada[l], row1(b_ada[l]))[:batch]
        shift1, scale1, gate1, shift2, scale2, gate2 = [per_batch(m) for m in
                                                        jnp.split(mod, 6, axis=-1)]
        wl = w_in[l]
        w_p = jnp.concatenate(
            [wl[:, :POOL_WIDTH],
             _pad_heads(wl[:, POOL_WIDTH:POOL_WIDTH + ATTN_WIDTH]),
             _pad_heads(wl[:, POOL_WIDTH + ATTN_WIDTH:POOL_WIDTH + 2 * ATTN_WIDTH]),
             _pad_heads(wl[:, POOL_WIDTH + 2 * ATTN_WIDTH:POOL_WIDTH + 3 * ATTN_WIDTH]),
             wl[:, POOL_WIDTH + 3 * ATTN_WIDTH:]], axis=1).astype(BF16)
        u_pool, qp, kp, vp, gates = _inproj(x2, shift1, scale1, row1(norm_mix_g[l]), w_p,
                                            cos_t, sa_t, sb_t, seq=seq, tm=256)
        attn = _moba(qp, kp, vp, bsel, batch=batch, seq=seq, group=4)
        wa_p = _pad_heads(w_branch_attn[l].T).T.astype(BF16)
        x1, h2t = _merge(u_pool, attn, gates, x2, pool_w[l].astype(BF16), row1(pool_scale[l]),
                         w_branch_pool[l].astype(BF16), wa_p, w_out[l].astype(BF16), gate1,
                         row1(norm_ffn_g[l]), shift2, scale2, seq=seq, tm=256)
        keys = peer_sub_keys[l].reshape(2 * PEER_HEADS, PEER_NKEYS, PEER_HALF)
        r2, e1, n, cden = _route(h2t, peer_wq[l].T.astype(BF16), keys, tm=256)
        x2 = _experts(h2t, peer_u[l].astype(BF16), peer_v[l].T.astype(BF16), r2, e1, n, cden,
                      x1, gate2, row1(norm_final_g), seq=seq, tm=512, sub_blocks=8)
    return x2.reshape(batch, seq, d)
```

```python
import functools
import math

import jax
import jax.numpy as jnp
import numpy as np
from jax import lax
from jax.experimental import pallas as pl
from jax.experimental.pallas import tpu as pltpu

F32 = jnp.float32
BF16 = jnp.bfloat16
HIGHEST = lax.Precision.HIGHEST

D_MODEL = 1024
POOL_WINDOWS = (2, 4, 8, 16)
POOL_WIDTH = 512
POOL_GROUP_DIM = 128
POOL_HALO = 16
ATTN_HEADS = 8
HEAD_DIM = 64
ATTN_WIDTH = 512
MOBA_BLOCK = 256
MOBA_TOPK = 3
ROPE_THETA = 500000.0
ROT_HALF = 8
PEER_HEADS = 8
PEER_NKEYS = 128
PEER_EXPERTS = PEER_NKEYS * PEER_NKEYS
PEER_HALF = 128
PEER_TOPK = 16
EPS = 1e-6

LANES = 128
SUBLANES = 8
HEAD_PAD = LANES
ATTN_PAD = ATTN_HEADS * HEAD_PAD
BIAS_LANE0 = HEAD_DIM
ONES_LANE = HEAD_DIM
MASK_BIG = 2.0 ** 100
CAUSAL_NEG = -1e30
NOT_MEMBER = 255.0
SQRT_HALF = float(np.sqrt(0.5).astype(np.float32))
VMEM_LIMIT = 56 * 1024 * 1024

IN_PROJ_PAD = POOL_WIDTH + 3 * ATTN_PAD + 2 * D_MODEL
Q_OFF = POOL_WIDTH
K_OFF = Q_OFF + ATTN_PAD
V_OFF = K_OFF + ATTN_PAD
G_OFF = V_OFF + ATTN_PAD


def _params(*semantics):
    return pltpu.CompilerParams(dimension_semantics=semantics, vmem_limit_bytes=VMEM_LIMIT)


def _const_spec(shape):
    nd = len(shape)
    return pl.BlockSpec(shape, lambda *_: (0,) * nd, pipeline_mode=pl.Buffered(1))


def _rms_modulate(x, g, shift, scale):
    ms = jnp.mean(x * x, axis=-1, keepdims=True)
    return (x * lax.rsqrt(ms + EPS) * g) * (1.0 + scale) + shift


def _ada_kernel(c_ref, w_ref, b_ref, o_ref):
    c = c_ref[...]
    act = c / (1.0 + jnp.exp(-c))
    o_ref[...] = jnp.dot(act, w_ref[...], precision=HIGHEST,
                         preferred_element_type=F32) + b_ref[...]


def _ada(c8, w, b):
    n = w.shape[1]
    tn = 1536
    return pl.pallas_call(
        _ada_kernel,
        grid=(n // tn,),
        in_specs=[pl.BlockSpec((8, D_MODEL), lambda j: (0, 0)),
                  pl.BlockSpec((D_MODEL, tn), lambda j: (0, j)),
                  pl.BlockSpec((1, tn), lambda j: (0, j))],
        out_specs=pl.BlockSpec((8, tn), lambda j: (0, j)),
        out_shape=jax.ShapeDtypeStruct((8, n), F32),
        compiler_params=_params("arbitrary"),
        name="ada_mod",
    )(c8, w, b)


def _inproj_kernel(x_ref, sh_ref, sc_ref, g_ref, w_ref, cos_ref, sa_ref, sb_ref,
                   u_ref, q_ref, k_ref, v_ref, gt_ref):
    h = _rms_modulate(x_ref[...], g_ref[...], sh_ref[0], sc_ref[0]).astype(BF16)

    def proj(off, width):
        return jnp.dot(h, w_ref[:, off:off + width], preferred_element_type=F32)

    u_ref[...] = proj(0, POOL_WIDTH)
    cos, sa, sb = cos_ref[...], sa_ref[...], sb_ref[...]

    def rope(t):
        return t * cos + pltpu.roll(t, LANES - ROT_HALF, 1) * sa + pltpu.roll(t, ROT_HALF, 1) * sb

    for hd in range(ATTN_HEADS):
        sl = slice(hd * HEAD_PAD, (hd + 1) * HEAD_PAD)
        q = rope(proj(Q_OFF + hd * HEAD_PAD, HEAD_PAD))
        q_ref[:, sl] = (q * (HEAD_DIM ** -0.5)).astype(BF16)
        k_ref[:, sl] = rope(proj(K_OFF + hd * HEAD_PAD, HEAD_PAD)).astype(BF16)
    v_ref[...] = proj(V_OFF, ATTN_PAD).astype(BF16)
    gt_ref[...] = 1.0 / (1.0 + jnp.exp(-proj(G_OFF, 2 * D_MODEL)))


def _inproj(x2, shift, scale, g, w_p, cos_t, sa_t, sb_t, *, seq, tm):
    t = x2.shape[0]
    tpb = seq // tm
    row = lambda i: (i, 0)
    per_batch = lambda i: (i // tpb, 0, 0)
    per_pos = lambda i: (i % tpb, 0)
    return pl.pallas_call(
        _inproj_kernel,
        grid=(t // tm,),
        in_specs=[pl.BlockSpec((tm, D_MODEL), row),
                  pl.BlockSpec((1, 1, D_MODEL), per_batch),
                  pl.BlockSpec((1, 1, D_MODEL), per_batch),
                  _const_spec((1, D_MODEL)),
                  _const_spec((D_MODEL, IN_PROJ_PAD)),
                  pl.BlockSpec((tm, LANES), per_pos),
                  pl.BlockSpec((tm, LANES), per_pos),
                  pl.BlockSpec((tm, LANES), per_pos)],
        out_specs=[pl.BlockSpec((tm, POOL_WIDTH), row),
                   pl.BlockSpec((tm, ATTN_PAD), row),
                   pl.BlockSpec((tm, ATTN_PAD), row),
                   pl.BlockSpec((tm, ATTN_PAD), row),
                   pl.BlockSpec((tm, 2 * D_MODEL), row)],
        out_shape=[jax.ShapeDtypeStruct((t, POOL_WIDTH), F32),
                   jax.ShapeDtypeStruct((t, ATTN_PAD), BF16),
                   jax.ShapeDtypeStruct((t, ATTN_PAD), BF16),
                   jax.ShapeDtypeStruct((t, ATTN_PAD), BF16),
                   jax.ShapeDtypeStruct((t, 2 * D_MODEL), F32)],
        compiler_params=_params("arbitrary"),
        name="in_proj",
    )(x2, shift, scale, g, w_p, cos_t, sa_t, sb_t)


def _moba_kernel(q_ref, k_ref, v_ref, bsel_ref, o_ref, qa_scr, ka_scr, va_scr, *, n_blocks,
                 group, heads):
    qi = pl.program_id(2)
    blk = MOBA_BLOCK
    lane = lax.broadcasted_iota(jnp.int32, (blk, LANES), 1)
    lane_f = lane.astype(F32)
    head_lanes = [slice(i * HEAD_PAD, (i + 1) * HEAD_PAD) for i in range(heads)]
    nt = (((1,), (1,)), ((), ()))

    @pl.when(qi == 0)
    def _prepare():
        kms = [jnp.dot(bsel_ref[...], k_ref[:, hl], preferred_element_type=F32)
               for hl in head_lanes]

        def per_block(nb, carry):
            rows = pl.ds(pl.multiple_of(nb * blk, blk), blk)
            valid = (lane >= BIAS_LANE0) & (lane < BIAS_LANE0 + nb)
            bias_lanes = (lane >= BIAS_LANE0) & (lane < BIAS_LANE0 + n_blocks)
            onehot = jnp.where(lane == BIAS_LANE0 + nb, 1.0, 0.0)
            for hl, km in zip(head_lanes, kms):
                q = q_ref[rows, hl].astype(F32)
                gate = lax.dot_general(q, km, nt, precision=HIGHEST,
                                       preferred_element_type=F32)
                work = jnp.where(valid, gate, -jnp.inf)
                chosen = jnp.zeros((blk, LANES), F32)
                for _ in range(MOBA_TOPK):
                    m = jnp.max(work, axis=1, keepdims=True)
                    idx = jnp.min(jnp.where(work == m, lane_f, 2.0 * LANES), axis=1,
                                  keepdims=True)
                    hit = (lane_f == idx) & valid
                    chosen = jnp.where(hit, 1.0, chosen)
                    work = jnp.where(hit, -jnp.inf, work)
                masked = bias_lanes & (chosen == 0.0)
                qa_scr[rows, hl] = jnp.where(masked, -MASK_BIG, q).astype(BF16)
                ka_scr[rows, hl] = jnp.where(lane < HEAD_DIM, k_ref[rows, hl].astype(F32),
                                             onehot).astype(BF16)
                va_scr[rows, hl] = jnp.where(lane == ONES_LANE, 1.0,
                                             v_ref[rows, hl].astype(F32)).astype(BF16)
            return carry

        lax.fori_loop(0, n_blocks, per_block, 0)

    own = pl.ds(pl.multiple_of(qi * blk, blk), blk)
    r_i = lax.broadcasted_iota(jnp.int32, (blk, blk), 0)
    c_i = lax.broadcasted_iota(jnp.int32, (blk, blk), 1)
    qas, init = [], []
    for hl in head_lanes:
        qa = qa_scr[own, hl]
        qas.append(qa)
        q_own = jnp.where(lane < HEAD_DIM, qa, jnp.zeros_like(qa))
        s = lax.dot_general(q_own, ka_scr[own, hl], nt, preferred_element_type=F32)
        s = jnp.where(c_i <= r_i, s, CAUSAL_NEG)
        m0 = jnp.max(s, axis=1, keepdims=True)
        p = jnp.exp(s - m0)
        init.append((m0, jnp.dot(p.astype(BF16), va_scr[own, hl],
                                 preferred_element_type=F32)))

    span = group * blk

    def past(j, carry):
        rows = pl.ds(pl.multiple_of(j * span, span), span)
        out = []
        for hl, qa, (m, acc) in zip(head_lanes, qas, carry):
            sj = lax.dot_general(qa, ka_scr[rows, hl], nt, preferred_element_type=F32)
            m_new = jnp.maximum(m, jnp.max(sj, axis=1, keepdims=True))
            pj = jnp.exp(sj - m_new)
            acc = acc * jnp.exp(m - m_new) + jnp.dot(pj.astype(BF16), va_scr[rows, hl],
                                                     preferred_element_type=F32)
            out.append((m_new, acc))
        return tuple(out)

    final = lax.fori_loop(0, (qi + group - 1) // group, past, tuple(init))
    for hl, (_, acc) in zip(head_lanes, final):
        o_ref[:, hl] = (acc / acc[:, ONES_LANE:ONES_LANE + 1]).astype(BF16)


def _moba(qp, kp, vp, bsel, *, batch, seq, group, heads):
    n_blocks = seq // MOBA_BLOCK
    assert n_blocks % group == 0 and ATTN_HEADS % heads == 0
    width = heads * HEAD_PAD
    head = lambda b, h, i: (b, h)
    return pl.pallas_call(
        functools.partial(_moba_kernel, n_blocks=n_blocks, group=group, heads=heads),
        grid=(batch, ATTN_HEADS // heads, n_blocks),
        in_specs=[pl.BlockSpec((seq, width), head),
                  pl.BlockSpec((seq, width), head),
                  pl.BlockSpec((seq, width), head),
                  _const_spec((LANES, seq))],
        out_specs=pl.BlockSpec((MOBA_BLOCK, width), lambda b, h, i: (b * n_blocks + i, h)),
        out_shape=jax.ShapeDtypeStruct((batch * seq, ATTN_PAD), BF16),
        scratch_shapes=[pltpu.VMEM((seq, width), BF16)] * 3,
        compiler_params=_params("arbitrary", "arbitrary", "arbitrary"),
        name="moba_attn",
    )(qp, kp, vp, bsel)


def _merge_kernel(u_ref, uprev_ref, attn_ref, gt_ref, x_ref, poolw_ref, pscale_ref, wp_ref,
                  wa_ref, wo_ref, gate1_ref, g2_ref, sh2_ref, sc2_ref, x1_ref, h2t_ref,
                  *, tm, tpb):
    i = pl.program_id(0)
    first = (i % tpb) == 0
    pos = lax.broadcasted_iota(jnp.int32, (tm, LANES), 0) + (i % tpb) * tm
    mixed = []
    for g, w in enumerate(POOL_WINDOWS):
        sl = slice(g * POOL_GROUP_DIM, (g + 1) * POOL_GROUP_DIM)
        u = u_ref[:, sl]
        halo = jnp.where(first, 0.0, uprev_ref[:, sl])
        ext = jnp.concatenate([halo, u], axis=0)
        span = 1
        while span < w:
            ext = ext + pltpu.roll(ext, span, 0)
            span *= 2
        cnt = jnp.minimum(pos + 1, w).astype(F32)
        diff = ext[POOL_HALO:, :] / cnt - u
        m = jnp.dot(diff.astype(BF16), poolw_ref[g], preferred_element_type=F32)
        mixed.append((m * pscale_ref[:, sl]).astype(BF16))
    pooled = jnp.concatenate(mixed, axis=1)
    bp = jnp.dot(pooled, wp_ref[...], preferred_element_type=F32)
    ba = jnp.dot(attn_ref[...], wa_ref[...], preferred_element_type=F32)
    merged = gt_ref[:, :D_MODEL] * bp + gt_ref[:, D_MODEL:] * ba
    y = jnp.dot(merged.astype(BF16), wo_ref[...], preferred_element_type=F32)
    x1 = x_ref[...] + gate1_ref[0] * y
    x1_ref[...] = x1
    h2 = _rms_modulate(x1, g2_ref[...], sh2_ref[0], sc2_ref[0])
    h2t_ref[...] = h2.T.astype(BF16)


def _merge(u_pool, attn, gates, x2, poolw, pscale, wp, wa, wo, gate1, g2, shift2, scale2,
           *, seq, tm):
    t = x2.shape[0]
    tpb = seq // tm
    row = lambda i: (i, 0)
    per_batch = lambda i: (i // tpb, 0, 0)
    halo_blocks = tm // POOL_HALO
    return pl.pallas_call(
        functools.partial(_merge_kernel, tm=tm, tpb=tpb),
        grid=(t // tm,),
        in_specs=[pl.BlockSpec((tm, POOL_WIDTH), row),
                  pl.BlockSpec((POOL_HALO, POOL_WIDTH),
                               lambda i: (jnp.maximum(i * halo_blocks - 1, 0), 0)),
                  pl.BlockSpec((tm, ATTN_PAD), row),
                  pl.BlockSpec((tm, 2 * D_MODEL), row),
                  pl.BlockSpec((tm, D_MODEL), row),
                  _const_spec((len(POOL_WINDOWS), POOL_GROUP_DIM, POOL_GROUP_DIM)),
                  _const_spec((1, POOL_WIDTH)),
                  _const_spec((POOL_WIDTH, D_MODEL)),
                  _const_spec((ATTN_PAD, D_MODEL)),
                  _const_spec((D_MODEL, D_MODEL)),
                  pl.BlockSpec((1, 1, D_MODEL), per_batch),
                  _const_spec((1, D_MODEL)),
                  pl.BlockSpec((1, 1, D_MODEL), per_batch),
                  pl.BlockSpec((1, 1, D_MODEL), per_batch)],
        out_specs=[pl.BlockSpec((tm, D_MODEL), row),
                   pl.BlockSpec((D_MODEL, tm), lambda i: (0, i))],
        out_shape=[jax.ShapeDtypeStruct((t, D_MODEL), F32),
                   jax.ShapeDtypeStruct((D_MODEL, t), BF16)],
        compiler_params=_params("arbitrary"),
        name="mixer_merge",
    )(u_pool, u_pool, attn, gates, x2, poolw, pscale, wp, wa, wo, gate1, g2, shift2, scale2)


def _extract_top16(streams):
    width = streams[0][0].shape[1]
    slot = lax.broadcasted_iota(jnp.int32, (PEER_TOPK, width), 0)

    def body(r, carry):
        out = []
        for (work, rank, vals), (_, index_f) in zip(carry, streams):
            m = jnp.max(work, axis=0, keepdims=True)
            idx = jnp.min(jnp.where(work == m, index_f, jnp.inf), axis=0, keepdims=True)
            hit = index_f == idx
            out.append((jnp.where(hit, -jnp.inf, work),
                        jnp.where(hit, lax.convert_element_type(r, F32), rank),
                        jnp.where(slot == r, m, vals)))
        return tuple(out)

    init = tuple((sc, jnp.full(sc.shape, NOT_MEMBER, F32), jnp.zeros((PEER_TOPK, width), F32))
                 for sc, _ in streams)
    return [(vals, rank) for _, rank, vals in lax.fori_loop(0, PEER_TOPK, body, init)]


def _drop_top16(streams):
    width = streams[0].shape[1]
    slot = lax.broadcasted_iota(jnp.int32, (PEER_TOPK, width), 0)

    def body(r, carry):
        out = []
        for work, vals in carry:
            m = jnp.max(work, axis=0, keepdims=True)
            out.append((jnp.where(work == m, -jnp.inf, work), jnp.where(slot == r, m, vals)))
        return tuple(out)

    init = tuple((sc, jnp.zeros((PEER_TOPK, width), F32)) for sc in streams)
    return [(vals, jnp.sum(jnp.where(work == -jnp.inf, 1.0, 0.0), axis=0, keepdims=True))
            for work, vals in lax.fori_loop(0, PEER_TOPK, body, init)]


def _route_kernel(h2t_ref, wqt_ref, keys_ref, r2_ref, e1_ref, n_ref, c_ref, qt_scr, *, tm):
    qt_scr[...] = jnp.dot(wqt_ref[...], h2t_ref[...], preferred_element_type=F32)
    half = PEER_TOPK // 2
    width = LANES

    def iota_f(rows):
        return lax.broadcasted_iota(jnp.int32, (rows, width), 0).astype(F32)

    tail = PEER_TOPK + (half - 1) * half
    slot = lax.broadcasted_iota(jnp.int32, (half, width), 0)

    def candidates(a, b):
        return jnp.concatenate([a[0:1, :] + b]
                               + [a[r:r + 1, :] + b[:half, :] for r in range(1, half)]
                               + [a[half:, :] + b[0:1, :]], axis=0)

    def staircase(chosen, a, b):
        e0 = jnp.exp(a - a[0:1, :])
        e1 = jnp.exp(b - b[0:1, :])
        n_lo = jnp.zeros((half, width), F32)
        pref_lo = jnp.zeros((half, width), F32)
        for r in range(half):
            lo = 0 if r == 0 else PEER_TOPK + (r - 1) * half
            cnt = PEER_TOPK if r == 0 else half
            grp = chosen[lo:lo + cnt, :]
            n_lo = jnp.where(slot == r, jnp.sum(grp, axis=0, keepdims=True), n_lo)
            pref_lo = jnp.where(slot == r, jnp.sum(grp * e1[:cnt, :], axis=0, keepdims=True),
                                pref_lo)
        n = jnp.concatenate([n_lo, chosen[tail:, :]], axis=0)
        pref = jnp.concatenate([pref_lo, chosen[tail:, :]], axis=0)
        return n, jnp.sum(e0 * pref, axis=0, keepdims=True)

    def emit(h, cols, s0, s1, a, b, rank2, n_dense, z):
        r2_ref[h, :, cols] = rank2
        e1_ref[h, :, cols] = jnp.exp(s1 - b[0:1, :])
        n_ref[h, :, cols] = n_dense
        c_ref[h, :, cols] = 0.5 * jnp.exp(s0 - a[0:1, :]) / z

    def per_head(h, carry):
        for cc in range(tm // width):
            route(h, slice(cc * width, (cc + 1) * width))
        return carry

    def route(h, cols):
        def half_scores(p):
            rows = pl.ds(pl.multiple_of((2 * h + p) * PEER_HALF, PEER_HALF), PEER_HALF)
            return jnp.dot(keys_ref[2 * h + p], qt_scr[rows, cols], precision=HIGHEST,
                           preferred_element_type=F32)

        s0 = half_scores(0)
        s1 = half_scores(1)

        (a, dropped0), (b, dropped1) = _drop_top16([s0, s1])
        cand = candidates(a, b)
        ((top, dropped),) = _drop_top16([cand])
        chosen = jnp.where(cand >= top[PEER_TOPK - 1:, :], 1.0, 0.0)
        n, z = staircase(chosen, a, b)
        rank2 = jnp.full((PEER_NKEYS, width), NOT_MEMBER, F32)
        n_dense = jnp.zeros((PEER_NKEYS, width), F32)
        for r in range(PEER_TOPK):
            rank2 = jnp.where(s1 == b[r:r + 1, :], float(r), rank2)
            n_dense = jnp.where(s0 == a[r:r + 1, :], n[r:r + 1, :], n_dense)
        emit(h, cols, s0, s1, a, b, rank2, n_dense, z)
        tie_free = jnp.min(jnp.where((dropped0 == PEER_TOPK) & (dropped1 == PEER_TOPK)
                                     & (dropped == PEER_TOPK), 1.0, 0.0))

        @pl.when(tie_free < 0.5)
        def _exact():
            key_f = iota_f(PEER_NKEYS)
            pair_f = jnp.concatenate(
                [iota_f(PEER_TOPK)]
                + [iota_f(half) + float(r * PEER_TOPK) for r in range(1, half)]
                + [(iota_f(half) + float(half)) * float(PEER_TOPK)], axis=0)
            (a, rank1), (b, rank2) = _extract_top16([(s0, key_f), (s1, key_f)])
            ((_, pick),) = _extract_top16([(candidates(a, b), pair_f)])
            n, z = staircase(jnp.where(pick < NOT_MEMBER, 1.0, 0.0), a, b)
            n_dense = jnp.zeros((PEER_NKEYS, width), F32)
            for r in range(PEER_TOPK):
                n_dense = jnp.where(rank1 == float(r), n[r:r + 1, :], n_dense)
            emit(h, cols, s0, s1, a, b, rank2, n_dense, z)

    lax.fori_loop(0, PEER_HEADS, per_head, 0)


def _route(h2t, wqt, keys, *, tm):
    t = h2t.shape[1]
    dense = jax.ShapeDtypeStruct((PEER_HEADS, PEER_NKEYS, t), F32)
    dense_spec = pl.BlockSpec((PEER_HEADS, PEER_NKEYS, tm), lambda i: (0, 0, i))
    return pl.pallas_call(
        functools.partial(_route_kernel, tm=tm),
        grid=(t // tm,),
        in_specs=[pl.BlockSpec((D_MODEL, tm), lambda i: (0, i)),
                  _const_spec((2 * PEER_HEADS * PEER_HALF, D_MODEL)),
                  _const_spec((2 * PEER_HEADS, PEER_NKEYS, PEER_HALF))],
        out_specs=[dense_spec] * 4,
        out_shape=[dense] * 4,
        scratch_shapes=[pltpu.VMEM((2 * PEER_HEADS * PEER_HALF, tm), F32)],
        compiler_params=_params("arbitrary"),
        name="peer_route",
    )(h2t, wqt, keys)


def _experts_kernel(h2t_ref, u_ref, vt_ref, r2_ref, e1_ref, n_ref, c_ref, x1_ref, gate2_ref,
                    gf_ref, o_ref, acc_scr, act_a, act_b, p_a, p_b, *, tm, sub_blocks):
    s = pl.program_id(1)

    @pl.when(s == 0)
    def _init():
        acc_scr[...] = jnp.zeros_like(acc_scr)
        act_b[...] = jnp.zeros_like(act_b)
        p_a[...] = jnp.zeros_like(p_a)

    quad = 4 * SUBLANES

    def step(act_new, act_old, p_new, p_old):
        n_cc = tm // LANES
        n_kq = PEER_NKEYS // quad
        n_piece = n_kq
        eb = act_new.shape[0]
        m1, k1 = eb // n_cc, D_MODEL // n_piece
        m2, k2 = D_MODEL // n_cc, eb // n_piece

        def matmul_pieces(cc, piece):
            u_rows = slice(cc * m1, (cc + 1) * m1)
            ks = slice(piece * k1, (piece + 1) * k1)
            part = jnp.dot(u_ref[u_rows, ks], h2t_ref[ks, :], preferred_element_type=F32)
            if piece == 0:
                act_new[u_rows, :] = part
            else:
                act_new[u_rows, :] += part
            d_rows = slice(cc * m2, (cc + 1) * m2)
            ks = slice(piece * k2, (piece + 1) * k2)
            acc_scr[d_rows, :] += jnp.dot(vt_ref[d_rows, ks], p_old[ks, :],
                                          preferred_element_type=F32)

        for cc in range(n_cc):
            cols = slice(cc * LANES, (cc + 1) * LANES)
            for kq in range(n_kq):
                matmul_pieces(cc, kq)
                i2 = slice(kq * quad, (kq + 1) * quad)
                w = [jnp.zeros((quad, LANES), F32) for _ in range(sub_blocks)]
                for h in range(PEER_HEADS):
                    r2 = r2_ref[h, i2, cols]
                    e1 = e1_ref[h, i2, cols]
                    for jb in range(sub_blocks):
                        keep = r2 < n_ref[h, jb:jb + 1, cols]
                        w[jb] = w[jb] + jnp.where(keep, e1 * c_ref[h, jb:jb + 1, cols], 0.0)
                for jb in range(sub_blocks):
                    rows = slice(jb * PEER_NKEYS + kq * quad, jb * PEER_NKEYS + (kq + 1) * quad)
                    a = act_old[rows, cols]
                    p_new[rows, cols] = (w[jb] * (a * (1.0 + lax.erf(a * SQRT_HALF)))).astype(BF16)

    @pl.when(s % 2 == 0)
    def _even():
        step(act_a, act_b, p_b, p_a)

    @pl.when(s % 2 == 1)
    def _odd():
        step(act_b, act_a, p_a, p_b)

    @pl.when(s == pl.num_programs(1) - 1)
    def _finish():
        x2 = x1_ref[...] + gate2_ref[0] * acc_scr[...].T
        ms = jnp.mean(x2 * x2, axis=-1, keepdims=True)
        o_ref[...] = x2 * lax.rsqrt(ms + EPS) * gf_ref[...]


def _experts(h2t, u_b, vt_b, r2, e1, n, c, x1, gate2, gf, *, seq, tm, sub_blocks):
    t = h2t.shape[1]
    tpb = seq // tm
    eb = sub_blocks * PEER_NKEYS
    n_eb = PEER_EXPERTS // eb
    tok3 = lambda i, s: (0, 0, i)
    blk = lambda s, lag: jnp.clip(s - lag, 0, n_eb - 1)
    return pl.pallas_call(
        functools.partial(_experts_kernel, tm=tm, sub_blocks=sub_blocks),
        grid=(t // tm, n_eb + 2),
        in_specs=[pl.BlockSpec((D_MODEL, tm), lambda i, s: (0, i)),
                  pl.BlockSpec((eb, D_MODEL), lambda i, s: (blk(s, 0), 0)),
                  pl.BlockSpec((D_MODEL, eb), lambda i, s: (0, blk(s, 2))),
                  pl.BlockSpec((PEER_HEADS, PEER_NKEYS, tm), tok3),
                  pl.BlockSpec((PEER_HEADS, PEER_NKEYS, tm), tok3),
                  pl.BlockSpec((PEER_HEADS, sub_blocks, tm), lambda i, s: (0, blk(s, 1), i)),
                  pl.BlockSpec((PEER_HEADS, sub_blocks, tm), lambda i, s: (0, blk(s, 1), i)),
                  pl.BlockSpec((tm, D_MODEL), lambda i, s: (i, 0)),
                  pl.BlockSpec((1, 1, D_MODEL), lambda i, s: (i // tpb, 0, 0)),
                  pl.BlockSpec((1, D_MODEL), lambda i, s: (0, 0))],
        out_specs=pl.BlockSpec((tm, D_MODEL), lambda i, s: (i, 0)),
        out_shape=jax.ShapeDtypeStruct((t, D_MODEL), F32),
        scratch_shapes=[pltpu.VMEM((D_MODEL, tm), F32),
                        pltpu.VMEM((eb, tm), F32), pltpu.VMEM((eb, tm), F32),
                        pltpu.VMEM((eb, tm), BF16), pltpu.VMEM((eb, tm), BF16)],
        compiler_params=_params("arbitrary", "arbitrary"),
        name="peer_experts",
    )(h2t, u_b, vt_b, r2, e1, n, c, x1, gate2, gf)


def _pad_heads(w):
    d = w.shape[0]
    w = w.reshape(d, ATTN_HEADS, HEAD_DIM)
    return jnp.pad(w, ((0, 0), (0, 0), (0, HEAD_PAD - HEAD_DIM))).reshape(d, ATTN_PAD)


def _rope_tables(seq):
    inv = ROPE_THETA ** (-jnp.arange(ROT_HALF, dtype=F32) / ROT_HALF)
    ang = jnp.arange(seq).astype(F32)[:, None] * inv[None, :]
    cos, sin = jnp.cos(ang), jnp.sin(ang)
    z = lambda n: jnp.zeros((seq, n), F32)
    cos_t = jnp.concatenate([cos, cos, jnp.ones((seq, LANES - 2 * ROT_HALF), F32)], axis=1)
    sa_t = jnp.concatenate([-sin, z(LANES - ROT_HALF)], axis=1)
    sb_t = jnp.concatenate([z(ROT_HALF), sin, z(LANES - 2 * ROT_HALF)], axis=1)
    return cos_t, sa_t, sb_t


def _block_mean_rows(seq):
    r = jnp.arange(LANES)[:, None]
    s = jnp.arange(seq)[None, :]
    return jnp.where(r - BIAS_LANE0 == s // MOBA_BLOCK, 1.0 / MOBA_BLOCK, 0.0).astype(BF16)


def kernel(x, c, w_ada, b_ada, norm_mix_g, w_in, pool_w, pool_scale, w_branch_pool,
           w_branch_attn, w_out, norm_ffn_g, peer_wq, peer_sub_keys, peer_u, peer_v,
           norm_final_g):
    batch, seq, d = x.shape
    depth = w_ada.shape[0]
    assert d == D_MODEL and batch <= 8 and seq % 512 == 0
    assert seq // MOBA_BLOCK <= LANES - BIAS_LANE0
    assert depth == 1
    t = batch * seq
    x2 = x.reshape(t, d)
    c8 = jnp.pad(c, ((0, 8 - batch), (0, 0)))
    cos_t, sa_t, sb_t = _rope_tables(seq)
    bsel = _block_mean_rows(seq)
    row1 = lambda v: v.reshape(1, -1)
    per_batch = lambda v: v.reshape(batch, 1, d)

    for l in range(depth):
        mod = _ada(c8, w_ada[l], row1(b_ada[l]))[:batch]
        shift1, scale1, gate1, shift2, scale2, gate2 = [per_batch(m) for m in
                                                        jnp.split(mod, 6, axis=-1)]
        wl = w_in[l]
        w_p = jnp.concatenate(
            [wl[:, :POOL_WIDTH],
             _pad_heads(wl[:, POOL_WIDTH:POOL_WIDTH + ATTN_WIDTH]),
             _pad_heads(wl[:, POOL_WIDTH + ATTN_WIDTH:POOL_WIDTH + 2 * ATTN_WIDTH]),
             _pad_heads(wl[:, POOL_WIDTH + 2 * ATTN_WIDTH:POOL_WIDTH + 3 * ATTN_WIDTH]),
             wl[:, POOL_WIDTH + 3 * ATTN_WIDTH:]], axis=1).astype(BF16)
        u_pool, qp, kp, vp, gates = _inproj(x2, shift1, scale1, row1(norm_mix_g[l]), w_p,
                                            cos_t, sa_t, sb_t, seq=seq, tm=256)
        attn = _moba(qp, kp, vp, bsel, batch=batch, seq=seq, group=4, heads=2)
        wa_p = _pad_heads(w_branch_attn[l].T).T.astype(BF16)
        x1, h2t = _merge(u_pool, attn, gates, x2, pool_w[l].astype(BF16), row1(pool_scale[l]),
                         w_branch_pool[l].astype(BF16), wa_p, w_out[l].astype(BF16), gate1,
                         row1(norm_ffn_g[l]), shift2, scale2, seq=seq, tm=256)
        keys = peer_sub_keys[l].reshape(2 * PEER_HEADS, PEER_NKEYS, PEER_HALF)
        r2, e1, n, cden = _route(h2t, peer_wq[l].T.astype(BF16), keys, tm=256)
        x2 = _experts(h2t, peer_u[l].astype(BF16), peer_v[l].T.astype(BF16), r2, e1, n, cden,
                      x1, gate2, row1(norm_final_g), seq=seq, tm=512, sub_blocks=8)
    return x2.reshape(batch, seq, d)
```

```python
import functools
import math

import jax
import jax.numpy as jnp
import numpy as np
from jax import lax
from jax.experimental import pallas as pl
from jax.experimental.pallas import tpu as pltpu

F32 = jnp.float32
BF16 = jnp.bfloat16
HIGHEST = lax.Precision.HIGHEST

D_MODEL = 1024
POOL_WINDOWS = (2, 4, 8, 16)
POOL_WIDTH = 512
POOL_GROUP_DIM = 128
POOL_HALO = 16
ATTN_HEADS = 8
HEAD_DIM = 64
ATTN_WIDTH = 512
MOBA_BLOCK = 256
MOBA_TOPK = 3
ROPE_THETA = 500000.0
ROT_HALF = 8
PEER_HEADS = 8
PEER_NKEYS = 128
PEER_EXPERTS = PEER_NKEYS * PEER_NKEYS
PEER_HALF = 128
PEER_TOPK = 16
EPS = 1e-6

LANES = 128
SUBLANES = 8
MXU_DEPTH = 256
HEAD_PAD = LANES
ATTN_PAD = ATTN_HEADS * HEAD_PAD
BIAS_LANE0 = HEAD_DIM
ONES_LANE = HEAD_DIM
MASK_BIG = 2.0 ** 100
CAUSAL_NEG = -1e30
NOT_MEMBER = 255.0
SQRT_HALF = float(np.sqrt(0.5).astype(np.float32))
VMEM_LIMIT = 56 * 1024 * 1024

IN_PROJ_PAD = POOL_WIDTH + 3 * ATTN_PAD + 2 * D_MODEL
Q_OFF = POOL_WIDTH
K_OFF = Q_OFF + ATTN_PAD
V_OFF = K_OFF + ATTN_PAD
G_OFF = V_OFF + ATTN_PAD


def _params(*semantics):
    return pltpu.CompilerParams(dimension_semantics=semantics, vmem_limit_bytes=VMEM_LIMIT)


def _const_spec(shape):
    nd = len(shape)
    return pl.BlockSpec(shape, lambda *_: (0,) * nd, pipeline_mode=pl.Buffered(1))


def _rms_modulate(x, g, shift, scale):
    ms = jnp.mean(x * x, axis=-1, keepdims=True)
    return (x * lax.rsqrt(ms + EPS) * g) * (1.0 + scale) + shift


def _ada_kernel(c_ref, w_ref, b_ref, o_ref):
    c = c_ref[...]
    act = c / (1.0 + jnp.exp(-c))
    o_ref[...] = jnp.dot(act, w_ref[...], precision=HIGHEST,
                         preferred_element_type=F32) + b_ref[...]


def _ada(c8, w, b):
    n = w.shape[1]
    tn = 1536
    return pl.pallas_call(
        _ada_kernel,
        grid=(n // tn,),
        in_specs=[pl.BlockSpec((8, D_MODEL), lambda j: (0, 0)),
                  pl.BlockSpec((D_MODEL, tn), lambda j: (0, j)),
                  pl.BlockSpec((1, tn), lambda j: (0, j))],
        out_specs=pl.BlockSpec((8, tn), lambda j: (0, j)),
        out_shape=jax.ShapeDtypeStruct((8, n), F32),
        compiler_params=_params("arbitrary"),
        name="ada_mod",
    )(c8, w, b)


def _inproj_kernel(x_ref, sh_ref, sc_ref, g_ref, w_ref, cos_ref, sa_ref, sb_ref,
                   u_ref, q_ref, k_ref, v_ref, gt_ref):
    h = _rms_modulate(x_ref[...], g_ref[...], sh_ref[0], sc_ref[0]).astype(BF16)

    def proj(off, width):
        return jnp.dot(h, w_ref[:, off:off + width], preferred_element_type=F32)

    u_ref[...] = proj(0, POOL_WIDTH)
    cos, sa, sb = cos_ref[...], sa_ref[...], sb_ref[...]

    def rope(t):
        return t * cos + pltpu.roll(t, LANES - ROT_HALF, 1) * sa + pltpu.roll(t, ROT_HALF, 1) * sb

    for hd in range(ATTN_HEADS):
        sl = slice(hd * HEAD_PAD, (hd + 1) * HEAD_PAD)
        q = rope(proj(Q_OFF + hd * HEAD_PAD, HEAD_PAD))
        q_ref[:, sl] = (q * (HEAD_DIM ** -0.5)).astype(BF16)
        k_ref[:, sl] = rope(proj(K_OFF + hd * HEAD_PAD, HEAD_PAD)).astype(BF16)
    v_ref[...] = proj(V_OFF, ATTN_PAD).astype(BF16)
    gt_ref[...] = 1.0 / (1.0 + jnp.exp(-proj(G_OFF, 2 * D_MODEL)))


def _inproj(x2, shift, scale, g, w_p, cos_t, sa_t, sb_t, *, seq, tm):
    t = x2.shape[0]
    tpb = seq // tm
    row = lambda i: (i, 0)
    per_batch = lambda i: (i // tpb, 0, 0)
    per_pos = lambda i: (i % tpb, 0)
    return pl.pallas_call(
        _inproj_kernel,
        grid=(t // tm,),
        in_specs=[pl.BlockSpec((tm, D_MODEL), row),
                  pl.BlockSpec((1, 1, D_MODEL), per_batch),
                  pl.BlockSpec((1, 1, D_MODEL), per_batch),
                  _const_spec((1, D_MODEL)),
                  _const_spec((D_MODEL, IN_PROJ_PAD)),
                  pl.BlockSpec((tm, LANES), per_pos),
                  pl.BlockSpec((tm, LANES), per_pos),
                  pl.BlockSpec((tm, LANES), per_pos)],
        out_specs=[pl.BlockSpec((tm, POOL_WIDTH), row),
                   pl.BlockSpec((tm, ATTN_PAD), row),
                   pl.BlockSpec((tm, ATTN_PAD), row),
                   pl.BlockSpec((tm, ATTN_PAD), row),
                   pl.BlockSpec((tm, 2 * D_MODEL), row)],
        out_shape=[jax.ShapeDtypeStruct((t, POOL_WIDTH), F32),
                   jax.ShapeDtypeStruct((t, ATTN_PAD), BF16),
                   jax.ShapeDtypeStruct((t, ATTN_PAD), BF16),
                   jax.ShapeDtypeStruct((t, ATTN_PAD), BF16),
                   jax.ShapeDtypeStruct((t, 2 * D_MODEL), F32)],
        compiler_params=_params("arbitrary"),
        name="in_proj",
    )(x2, shift, scale, g, w_p, cos_t, sa_t, sb_t)


def _moba_kernel(q_ref, k_ref, v_ref, bsel_ref, o_ref, qa_scr, ka_scr, va_scr, *, n_blocks,
                 group, heads):
    qi = pl.program_id(2)
    blk = MOBA_BLOCK
    lane = lax.broadcasted_iota(jnp.int32, (blk, LANES), 1)
    lane_f = lane.astype(F32)
    head_lanes = [slice(i * HEAD_PAD, (i + 1) * HEAD_PAD) for i in range(heads)]
    nt = (((1,), (1,)), ((), ()))

    @pl.when(qi == 0)
    def _prepare():
        kms = [jnp.dot(bsel_ref[...], k_ref[:, hl], preferred_element_type=F32)
               for hl in head_lanes]

        def per_block(nb, carry):
            rows = pl.ds(pl.multiple_of(nb * blk, blk), blk)
            valid = (lane >= BIAS_LANE0) & (lane < BIAS_LANE0 + nb)
            bias_lanes = (lane >= BIAS_LANE0) & (lane < BIAS_LANE0 + n_blocks)
            onehot = jnp.where(lane == BIAS_LANE0 + nb, 1.0, 0.0)
            qs = [q_ref[rows, hl].astype(F32) for hl in head_lanes]
            works = [jnp.where(valid, lax.dot_general(q, km, nt, precision=HIGHEST,
                                                      preferred_element_type=F32), -jnp.inf)
                     for q, km in zip(qs, kms)]
            chosen = [jnp.zeros((blk, LANES), F32) for _ in head_lanes]
            for _ in range(MOBA_TOPK):
                ms = [jnp.max(w, axis=1, keepdims=True) for w in works]
                idxs = [jnp.min(jnp.where(w == m, lane_f, 2.0 * LANES), axis=1, keepdims=True)
                        for w, m in zip(works, ms)]
                hits = [(lane_f == idx) & valid for idx in idxs]
                chosen = [jnp.where(hit, 1.0, ch) for hit, ch in zip(hits, chosen)]
                works = [jnp.where(hit, -jnp.inf, w) for hit, w in zip(hits, works)]
            for hl, q, ch in zip(head_lanes, qs, chosen):
                masked = bias_lanes & (ch == 0.0)
                qa_scr[rows, hl] = jnp.where(masked, -MASK_BIG, q).astype(BF16)
                ka_scr[rows, hl] = jnp.where(lane < HEAD_DIM, k_ref[rows, hl].astype(F32),
                                             onehot).astype(BF16)
                va_scr[rows, hl] = jnp.where(lane == ONES_LANE, 1.0,
                                             v_ref[rows, hl].astype(F32)).astype(BF16)
            return carry

        lax.fori_loop(0, n_blocks, per_block, 0)

    own = pl.ds(pl.multiple_of(qi * blk, blk), blk)
    r_i = lax.broadcasted_iota(jnp.int32, (blk, blk), 0)
    c_i = lax.broadcasted_iota(jnp.int32, (blk, blk), 1)
    qas = [qa_scr[own, hl] for hl in head_lanes]
    own_s = [jnp.where(c_i <= r_i,
                       lax.dot_general(jnp.where(lane < HEAD_DIM, qa, jnp.zeros_like(qa)),
                                       ka_scr[own, hl], nt, preferred_element_type=F32),
                       CAUSAL_NEG) for hl, qa in zip(head_lanes, qas)]
    own_m = [jnp.max(s, axis=1, keepdims=True) for s in own_s]
    init = [(m0, jnp.dot(jnp.exp(s - m0).astype(BF16), va_scr[own, hl],
                         preferred_element_type=F32))
            for hl, s, m0 in zip(head_lanes, own_s, own_m)]

    span = group * blk

    def past(j, carry):
        rows = pl.ds(pl.multiple_of(j * span, span), span)
        scores = [lax.dot_general(qa, ka_scr[rows, hl], nt, preferred_element_type=F32)
                  for hl, qa in zip(head_lanes, qas)]
        m_new = [jnp.maximum(m, jnp.max(sj, axis=1, keepdims=True))
                 for sj, (m, _) in zip(scores, carry)]
        out = []
        for hl, sj, mn, (m, acc) in zip(head_lanes, scores, m_new, carry):
            pj = jnp.exp(sj - mn)
            acc = acc * jnp.exp(m - mn) + jnp.dot(pj.astype(BF16), va_scr[rows, hl],
                                                  preferred_element_type=F32)
            out.append((mn, acc))
        return tuple(out)

    final = lax.fori_loop(0, (qi + group - 1) // group, past, tuple(init))
    for hl, (_, acc) in zip(head_lanes, final):
        o_ref[:, hl] = (acc / acc[:, ONES_LANE:ONES_LANE + 1]).astype(BF16)


def _moba(qp, kp, vp, bsel, *, batch, seq, group, heads):
    n_blocks = seq // MOBA_BLOCK
    assert n_blocks % group == 0 and ATTN_HEADS % heads == 0
    width = heads * HEAD_PAD
    head = lambda b, h, i: (b, h)
    return pl.pallas_call(
        functools.partial(_moba_kernel, n_blocks=n_blocks, group=group, heads=heads),
        grid=(batch, ATTN_HEADS // heads, n_blocks),
        in_specs=[pl.BlockSpec((seq, width), head),
                  pl.BlockSpec((seq, width), head),
                  pl.BlockSpec((seq, width), head),
                  _const_spec((LANES, seq))],
        out_specs=pl.BlockSpec((MOBA_BLOCK, width), lambda b, h, i: (b * n_blocks + i, h)),
        out_shape=jax.ShapeDtypeStruct((batch * seq, ATTN_PAD), BF16),
        scratch_shapes=[pltpu.VMEM((seq, width), BF16)] * 3,
        compiler_params=_params("arbitrary", "arbitrary", "arbitrary"),
        name="moba_attn",
    )(qp, kp, vp, bsel)


def _merge_kernel(u_ref, uprev_ref, attn_ref, gt_ref, x_ref, poolw_ref, pscale_ref, wp_ref,
                  wa_ref, wo_ref, gate1_ref, g2_ref, sh2_ref, sc2_ref, x1_ref, h2t_ref,
                  *, tm, tpb):
    i = pl.program_id(0)
    first = (i % tpb) == 0
    pos = lax.broadcasted_iota(jnp.int32, (tm, LANES), 0) + (i % tpb) * tm
    mixed = []
    for g, w in enumerate(POOL_WINDOWS):
        sl = slice(g * POOL_GROUP_DIM, (g + 1) * POOL_GROUP_DIM)
        u = u_ref[:, sl]
        halo = jnp.where(first, 0.0, uprev_ref[:, sl])
        ext = jnp.concatenate([halo, u], axis=0)
        span = 1
        while span < w:
            ext = ext + pltpu.roll(ext, span, 0)
            span *= 2
        cnt = jnp.minimum(pos + 1, w).astype(F32)
        diff = ext[POOL_HALO:, :] / cnt - u
        m = jnp.dot(diff.astype(BF16), poolw_ref[g], preferred_element_type=F32)
        mixed.append((m * pscale_ref[:, sl]).astype(BF16))
    pooled = jnp.concatenate(mixed, axis=1)
    bp = jnp.dot(pooled, wp_ref[...], preferred_element_type=F32)
    ba = jnp.dot(attn_ref[...], wa_ref[...], preferred_element_type=F32)
    merged = gt_ref[:, :D_MODEL] * bp + gt_ref[:, D_MODEL:] * ba
    y = jnp.dot(merged.astype(BF16), wo_ref[...], preferred_element_type=F32)
    x1 = x_ref[...] + gate1_ref[0] * y
    x1_ref[...] = x1
    h2 = _rms_modulate(x1, g2_ref[...], sh2_ref[0], sc2_ref[0])
    h2t_ref[...] = h2.T.astype(BF16)


def _merge(u_pool, attn, gates, x2, poolw, pscale, wp, wa, wo, gate1, g2, shift2, scale2,
           *, seq, tm):
    t = x2.shape[0]
    tpb = seq // tm
    row = lambda i: (i, 0)
    per_batch = lambda i: (i // tpb, 0, 0)
    halo_blocks = tm // POOL_HALO
    return pl.pallas_call(
        functools.partial(_merge_kernel, tm=tm, tpb=tpb),
        grid=(t // tm,),
        in_specs=[pl.BlockSpec((tm, POOL_WIDTH), row),
                  pl.BlockSpec((POOL_HALO, POOL_WIDTH),
                               lambda i: (jnp.maximum(i * halo_blocks - 1, 0), 0)),
                  pl.BlockSpec((tm, ATTN_PAD), row),
                  pl.BlockSpec((tm, 2 * D_MODEL), row),
                  pl.BlockSpec((tm, D_MODEL), row),
                  _const_spec((len(POOL_WINDOWS), POOL_GROUP_DIM, POOL_GROUP_DIM)),
                  _const_spec((1, POOL_WIDTH)),
                  _const_spec((POOL_WIDTH, D_MODEL)),
                  _const_spec((ATTN_PAD, D_MODEL)),
                  _const_spec((D_MODEL, D_MODEL)),
                  pl.BlockSpec((1, 1, D_MODEL), per_batch),
                  _const_spec((1, D_MODEL)),
                  pl.BlockSpec((1, 1, D_MODEL), per_batch),
                  pl.BlockSpec((1, 1, D_MODEL), per_batch)],
        out_specs=[pl.BlockSpec((tm, D_MODEL), row),
                   pl.BlockSpec((D_MODEL, tm), lambda i: (0, i))],
        out_shape=[jax.ShapeDtypeStruct((t, D_MODEL), F32),
                   jax.ShapeDtypeStruct((D_MODEL, t), BF16)],
        compiler_params=_params("arbitrary"),
        name="mixer_merge",
    )(u_pool, u_pool, attn, gates, x2, poolw, pscale, wp, wa, wo, gate1, g2, shift2, scale2)


def _extract_top16(streams):
    width = streams[0][0].shape[1]
    slot = lax.broadcasted_iota(jnp.int32, (PEER_TOPK, width), 0)

    def body(r, carry):
        out = []
        for (work, rank, vals), (_, index_f) in zip(carry, streams):
            m = jnp.max(work, axis=0, keepdims=True)
            idx = jnp.min(jnp.where(work == m, index_f, jnp.inf), axis=0, keepdims=True)
            hit = index_f == idx
            out.append((jnp.where(hit, -jnp.inf, work),
                        jnp.where(hit, lax.convert_element_type(r, F32), rank),
                        jnp.where(slot == r, m, vals)))
        return tuple(out)

    init = tuple((sc, jnp.full(sc.shape, NOT_MEMBER, F32), jnp.zeros((PEER_TOPK, width), F32))
                 for sc, _ in streams)
    return [(vals, rank) for _, rank, vals in lax.fori_loop(0, PEER_TOPK, body, init)]


def _batcher_pairs(n):
    pairs, p = [], 1
    while p < n:
        k = p
        while k >= 1:
            for j in range(k % p, n - k, 2 * k):
                for i in range(min(k, n - j - k)):
                    if (i + j) // (2 * p) == (i + j + k) // (2 * p):
                        pairs.append((i + j, i + j + k))
            k //= 2
        p *= 2
    return pairs


def _sorted_top16(regs):
    x = list(regs)

    def exchange(i, j):
        hi, lo = x[i], x[j]
        if lo is None:
            return
        if hi is None:
            x[i], x[j] = lo, None
        else:
            x[i], x[j] = jnp.maximum(hi, lo), jnp.minimum(hi, lo)

    for i, j in _batcher_pairs(PEER_TOPK):
        exchange(i, j)
    for shift in (4, 2, 1):
        other = [None if v is None else pltpu.roll(v, shift, 0) for v in x]
        for i in range(PEER_TOPK):
            mine, theirs = x[i], other[PEER_TOPK - 1 - i]
            x[i] = theirs if mine is None else mine if theirs is None else jnp.maximum(mine, theirs)
        dist = PEER_TOPK // 2
        while dist >= 1:
            for i in range(PEER_TOPK):
                if i & dist == 0:
                    exchange(i, i + dist)
            dist //= 2
    return x


def _route_kernel(h2t_ref, wqt_ref, keys_ref, r2_ref, e1_ref, n_ref, c_ref, qt_scr, *, tm):
    qt_scr[...] = jnp.dot(wqt_ref[...], h2t_ref[...], preferred_element_type=F32)
    half = PEER_TOPK // 2
    width = LANES

    def iota_f(rows):
        return lax.broadcasted_iota(jnp.int32, (rows, width), 0).astype(F32)

    tail = PEER_TOPK + (half - 1) * half
    slot = lax.broadcasted_iota(jnp.int32, (half, width), 0)

    def candidates(a, b):
        return jnp.concatenate([a[0:1, :] + b]
                               + [a[r:r + 1, :] + b[:half, :] for r in range(1, half)]
                               + [a[half:, :] + b[0:1, :]], axis=0)

    def staircase(chosen, a, b):
        e0 = jnp.exp(a - a[0:1, :])
        e1 = jnp.exp(b - b[0:1, :])
        n_lo = jnp.zeros((half, width), F32)
        pref_lo = jnp.zeros((half, width), F32)
        for r in range(half):
            lo = 0 if r == 0 else PEER_TOPK + (r - 1) * half
            cnt = PEER_TOPK if r == 0 else half
            grp = chosen[lo:lo + cnt, :]
            n_lo = jnp.where(slot == r, jnp.sum(grp, axis=0, keepdims=True), n_lo)
            pref_lo = jnp.where(slot == r, jnp.sum(grp * e1[:cnt, :], axis=0, keepdims=True),
                                pref_lo)
        n = jnp.concatenate([n_lo, chosen[tail:, :]], axis=0)
        pref = jnp.concatenate([pref_lo, chosen[tail:, :]], axis=0)
        return n, jnp.sum(e0 * pref, axis=0, keepdims=True)

    def emit(h, cols, s0, s1, a, b, rank2, n_dense, z):
        r2_ref[h, :, cols] = rank2
        e1_ref[h, :, cols] = jnp.exp(s1 - b[0:1, :])
        n_ref[h, :, cols] = n_dense
        c_ref[h, :, cols] = 0.5 * jnp.exp(s0 - a[0:1, :]) / z

    def per_head(h, carry):
        for cc in range(tm // width):
            route(h, slice(cc * width, (cc + 1) * width))
        return carry

    def route(h, cols):
        def half_scores(p):
            rows = pl.ds(pl.multiple_of((2 * h + p) * PEER_HALF, PEER_HALF), PEER_HALF)
            return jnp.dot(keys_ref[2 * h + p], qt_scr[rows, cols], precision=HIGHEST,
                           preferred_element_type=F32)

        s0 = half_scores(0)
        s1 = half_scores(1)

        pieces = PEER_NKEYS // SUBLANES
        s0v = [s0[v * SUBLANES:(v + 1) * SUBLANES, :] for v in range(pieces)]
        s1v = [s1[v * SUBLANES:(v + 1) * SUBLANES, :] for v in range(pieces)]
        a = _sorted_top16(s0v)
        b = _sorted_top16(s1v)
        sub = lax.broadcasted_iota(jnp.int32, (SUBLANES, width), 0)

        def stack(vals):
            out = vals[0]
            for j in range(1, SUBLANES):
                out = jnp.where(sub == j, vals[j], out)
            return out

        def total(x):
            return jnp.broadcast_to(jnp.sum(x, axis=0, keepdims=True), (SUBLANES, width))

        b_lo, b_hi, a_hi = stack(b[:half]), stack(b[half:]), stack(a[half:])
        cand = ([a[0] + b_lo, a[0] + b_hi] + [a[r] + b_lo for r in range(1, half)]
                + [a_hi + b[0]])
        top = _sorted_top16(cand + [None] * (PEER_TOPK - len(cand)))
        chosen = [jnp.where(c >= top[-1], 1.0, 0.0) for c in cand]
        n = ([total(chosen[0] + chosen[1])] + [total(chosen[1 + r]) for r in range(1, half)]
             + [total(jnp.where(sub == j, chosen[-1], 0.0)) for j in range(half)])
        z = total(sum(ch * jnp.exp(c - top[0]) for ch, c in zip(chosen, cand)))
        rank2, n_dense = [], []
        for v in range(pieces):
            r2, nd = jnp.full((SUBLANES, width), NOT_MEMBER, F32), jnp.zeros((SUBLANES, width), F32)
            for r in range(PEER_TOPK):
                r2 = jnp.where(s1v[v] == b[r], float(r), r2)
                nd = jnp.where(s0v[v] == a[r], n[r], nd)
            rank2.append(r2)
            n_dense.append(nd)
        emit(h, cols, s0, s1, a[0], b[0], jnp.concatenate(rank2, axis=0),
             jnp.concatenate(n_dense, axis=0), z[0:1, :])

        distinct = jnp.ones((SUBLANES, width), F32)
        for lst in (a, b, top):
            for r in range(PEER_TOPK - 1):
                distinct = jnp.where(lst[r] == lst[r + 1], 0.0, distinct)
        at_least = lambda regs, thr: total(sum(jnp.where(x >= thr, 1.0, 0.0) for x in regs))
        sixteen = ((at_least(s0v, a[-1]) == PEER_TOPK) & (at_least(s1v, b[-1]) == PEER_TOPK)
                   & (total(sum(chosen)) == PEER_TOPK))
        tie_free = jnp.min(jnp.where(sixteen, distinct, 0.0))

        @pl.when(tie_free < 0.5)
        def _exact():
            key_f = iota_f(PEER_NKEYS)
            pair_f = jnp.concatenate(
                [iota_f(PEER_TOPK)]
                + [iota_f(half) + float(r * PEER_TOPK) for r in range(1, half)]
                + [(iota_f(half) + float(half)) * float(PEER_TOPK)], axis=0)
            (a, rank1), (b, rank2) = _extract_top16([(s0, key_f), (s1, key_f)])
            ((_, pick),) = _extract_top16([(candidates(a, b), pair_f)])
            n, z = staircase(jnp.where(pick < NOT_MEMBER, 1.0, 0.0), a, b)
            n_dense = jnp.zeros((PEER_NKEYS, width), F32)
            for r in range(PEER_TOPK):
                n_dense = jnp.where(rank1 == float(r), n[r:r + 1, :], n_dense)
            emit(h, cols, s0, s1, a, b, rank2, n_dense, z)

    lax.fori_loop(0, PEER_HEADS, per_head, 0)


def _route(h2t, wqt, keys, *, tm):
    t = h2t.shape[1]
    dense = jax.ShapeDtypeStruct((PEER_HEADS, PEER_NKEYS, t), F32)
    dense_spec = pl.BlockSpec((PEER_HEADS, PEER_NKEYS, tm), lambda i: (0, 0, i))
    return pl.pallas_call(
        functools.partial(_route_kernel, tm=tm),
        grid=(t // tm,),
        in_specs=[pl.BlockSpec((D_MODEL, tm), lambda i: (0, i)),
                  _const_spec((2 * PEER_HEADS * PEER_HALF, D_MODEL)),
                  _const_spec((2 * PEER_HEADS, PEER_NKEYS, PEER_HALF))],
        out_specs=[dense_spec] * 4,
        out_shape=[dense] * 4,
        scratch_shapes=[pltpu.VMEM((2 * PEER_HEADS * PEER_HALF, tm), F32)],
        compiler_params=_params("arbitrary"),
        name="peer_route",
    )(h2t, wqt, keys)


def _experts_kernel(h2t_ref, u_ref, vt_ref, r2_ref, e1_ref, n_ref, c_ref, x1_ref, gate2_ref,
                    gf_ref, o_ref, acc_scr, act_a, act_b, p_a, p_b, *, tm, sub_blocks):
    s = pl.program_id(1)

    @pl.when(s == 0)
    def _init():
        acc_scr[...] = jnp.zeros_like(acc_scr)
        act_b[...] = jnp.zeros_like(act_b)
        p_a[...] = jnp.zeros_like(p_a)

    quad = 4 * SUBLANES

    def step(act_new, act_old, p_new, p_old):
        n_cc = tm // LANES
        n_kq = PEER_NKEYS // quad
        eb = act_new.shape[0]
        k_split = D_MODEL // MXU_DEPTH
        m_split = 2 * n_cc * n_kq // k_split
        m1, k1 = eb // m_split, D_MODEL // k_split
        m2, k2 = D_MODEL // m_split, eb // k_split

        def matmul_pieces(piece):
            mi, ki = divmod(piece, k_split)
            u_rows = slice(mi * m1, (mi + 1) * m1)
            ks = slice(ki * k1, (ki + 1) * k1)
            part = jnp.dot(u_ref[u_rows, ks], h2t_ref[ks, :], preferred_element_type=F32)
            if ki == 0:
                act_new[u_rows, :] = part
            else:
                act_new[u_rows, :] += part
            d_rows = slice(mi * m2, (mi + 1) * m2)
            ks = slice(ki * k2, (ki + 1) * k2)
            acc_scr[d_rows, :] += jnp.dot(vt_ref[d_rows, ks], p_old[ks, :],
                                          preferred_element_type=F32)

        for cc in range(n_cc):
            cols = slice(cc * LANES, (cc + 1) * LANES)
            for kq in range(n_kq):
                block = cc * n_kq + kq
                matmul_pieces(2 * block)
                i2 = slice(kq * quad, (kq + 1) * quad)
                w = [jnp.zeros((quad, LANES), F32) for _ in range(sub_blocks)]
                for h in range(PEER_HEADS):
                    r2 = r2_ref[h, i2, cols]
                    e1 = e1_ref[h, i2, cols]
                    for jb in range(sub_blocks):
                        keep = r2 < n_ref[h, jb:jb + 1, cols]
                        w[jb] = w[jb] + jnp.where(keep, e1 * c_ref[h, jb:jb + 1, cols], 0.0)
                matmul_pieces(2 * block + 1)
                for jb in range(sub_blocks):
                    rows = slice(jb * PEER_NKEYS + kq * quad, jb * PEER_NKEYS + (kq + 1) * quad)
                    a = act_old[rows, cols]
                    p_new[rows, cols] = (w[jb] * (a * (1.0 + lax.erf(a * SQRT_HALF)))).astype(BF16)

    @pl.when(s % 2 == 0)
    def _even():
        step(act_a, act_b, p_b, p_a)

    @pl.when(s % 2 == 1)
    def _odd():
        step(act_b, act_a, p_a, p_b)

    @pl.when(s == pl.num_programs(1) - 1)
    def _finish():
        x2 = x1_ref[...] + gate2_ref[0] * acc_scr[...].T
        ms = jnp.mean(x2 * x2, axis=-1, keepdims=True)
        o_ref[...] = x2 * lax.rsqrt(ms + EPS) * gf_ref[...]


def _experts(h2t, u_b, vt_b, r2, e1, n, c, x1, gate2, gf, *, seq, tm, sub_blocks):
    t = h2t.shape[1]
    tpb = seq // tm
    eb = sub_blocks * PEER_NKEYS
    n_eb = PEER_EXPERTS // eb
    tok3 = lambda i, s: (0, 0, i)
    blk = lambda s, lag: jnp.clip(s - lag, 0, n_eb - 1)
    return pl.pallas_call(
        functools.partial(_experts_kernel, tm=tm, sub_blocks=sub_blocks),
        grid=(t // tm, n_eb + 2),
        in_specs=[pl.BlockSpec((D_MODEL, tm), lambda i, s: (0, i)),
                  pl.BlockSpec((eb, D_MODEL), lambda i, s: (blk(s, 0), 0)),
                  pl.BlockSpec((D_MODEL, eb), lambda i, s: (0, blk(s, 2))),
                  pl.BlockSpec((PEER_HEADS, PEER_NKEYS, tm), tok3),
                  pl.BlockSpec((PEER_HEADS, PEER_NKEYS, tm), tok3),
                  pl.BlockSpec((PEER_HEADS, sub_blocks, tm), lambda i, s: (0, blk(s, 1), i)),
                  pl.BlockSpec((PEER_HEADS, sub_blocks, tm), lambda i, s: (0, blk(s, 1), i)),
                  pl.BlockSpec((tm, D_MODEL), lambda i, s: (i, 0)),
                  pl.BlockSpec((1, 1, D_MODEL), lambda i, s: (i // tpb, 0, 0)),
                  pl.BlockSpec((1, D_MODEL), lambda i, s: (0, 0))],
        out_specs=pl.BlockSpec((tm, D_MODEL), lambda i, s: (i, 0)),
        out_shape=jax.ShapeDtypeStruct((t, D_MODEL), F32),
        scratch_shapes=[pltpu.VMEM((D_MODEL, tm), F32),
                        pltpu.VMEM((eb, tm), F32), pltpu.VMEM((eb, tm), F32),
                        pltpu.VMEM((eb, tm), BF16), pltpu.VMEM((eb, tm), BF16)],
        compiler_params=_params("arbitrary", "arbitrary"),
        name="peer_experts",
    )(h2t, u_b, vt_b, r2, e1, n, c, x1, gate2, gf)


def _pad_heads(w):
    d = w.shape[0]
    w = w.reshape(d, ATTN_HEADS, HEAD_DIM)
    return jnp.pad(w, ((0, 0), (0, 0), (0, HEAD_PAD - HEAD_DIM))).reshape(d, ATTN_PAD)


def _rope_tables(seq):
    inv = ROPE_THETA ** (-jnp.arange(ROT_HALF, dtype=F32) / ROT_HALF)
    ang = jnp.arange(seq).astype(F32)[:, None] * inv[None, :]
    cos, sin = jnp.cos(ang), jnp.sin(ang)
    z = lambda n: jnp.zeros((seq, n), F32)
    cos_t = jnp.concatenate([cos, cos, jnp.ones((seq, LANES - 2 * ROT_HALF), F32)], axis=1)
    sa_t = jnp.concatenate([-sin, z(LANES - ROT_HALF)], axis=1)
    sb_t = jnp.concatenate([z(ROT_HALF), sin, z(LANES - 2 * ROT_HALF)], axis=1)
    return cos_t, sa_t, sb_t


def _block_mean_rows(seq):
    r = jnp.arange(LANES)[:, None]
    s = jnp.arange(seq)[None, :]
    return jnp.where(r - BIAS_LANE0 == s // MOBA_BLOCK, 1.0 / MOBA_BLOCK, 0.0).astype(BF16)


def kernel(x, c, w_ada, b_ada, norm_mix_g, w_in, pool_w, pool_scale, w_branch_pool,
           w_branch_attn, w_out, norm_ffn_g, peer_wq, peer_sub_keys, peer_u, peer_v,
           norm_final_g):
    batch, seq, d = x.shape
    depth = w_ada.shape[0]
    assert d == D_MODEL and batch <= 8 and seq % 512 == 0
    assert seq // MOBA_BLOCK <= LANES - BIAS_LANE0
    assert depth == 1
    t = batch * seq
    x2 = x.reshape(t, d)
    c8 = jnp.pad(c, ((0, 8 - batch), (0, 0)))
    cos_t, sa_t, sb_t = _rope_tables(seq)
    bsel = _block_mean_rows(seq)
    row1 = lambda v: v.reshape(1, -1)
    per_batch = lambda v: v.reshape(batch, 1, d)

    for l in range(depth):
        mod = _ada(c8, w_ada[l], row1(b_ada[l]))[:batch]
        shift1, scale1, gate1, shift2, scale2, gate2 = [per_batch(m) for m in
                                                        jnp.split(mod, 6, axis=-1)]
        wl = w_in[l]
        w_p = jnp.concatenate(
            [wl[:, :POOL_WIDTH],
             _pad_heads(wl[:, POOL_WIDTH:POOL_WIDTH + ATTN_WIDTH]),
             _pad_heads(wl[:, POOL_WIDTH + ATTN_WIDTH:POOL_WIDTH + 2 * ATTN_WIDTH]),
             _pad_heads(wl[:, POOL_WIDTH + 2 * ATTN_WIDTH:POOL_WIDTH + 3 * ATTN_WIDTH]),
             wl[:, POOL_WIDTH + 3 * ATTN_WIDTH:]], axis=1).astype(BF16)
        u_pool, qp, kp, vp, gates = _inproj(x2, shift1, scale1, row1(norm_mix_g[l]), w_p,
                                            cos_t, sa_t, sb_t, seq=seq, tm=256)
        attn = _moba(qp, kp, vp, bsel, batch=batch, seq=seq, group=4, heads=2)
        wa_p = _pad_heads(w_branch_attn[l].T).T.astype(BF16)
        x1, h2t = _merge(u_pool, attn, gates, x2, pool_w[l].astype(BF16), row1(pool_scale[l]),
                         w_branch_pool[l].astype(BF16), wa_p, w_out[l].astype(BF16), gate1,
                         row1(norm_ffn_g[l]), shift2, scale2, seq=seq, tm=256)
        keys = peer_sub_keys[l].reshape(2 * PEER_HEADS, PEER_NKEYS, PEER_HALF)
        r2, e1, n, cden = _route(h2t, peer_wq[l].T.astype(BF16), keys, tm=256)
        x2 = _experts(h2t, peer_u[l].astype(BF16), peer_v[l].T.astype(BF16), r2, e1, n, cden,
                      x1, gate2, row1(norm_final_g), seq=seq, tm=512, sub_blocks=8)
    return x2.reshape(batch, seq, d)
```

```python
import functools
import math

import jax
import jax.numpy as jnp
import numpy as np
from jax import lax
from jax.experimental import pallas as pl
from jax.experimental.pallas import tpu as pltpu

F32 = jnp.float32
BF16 = jnp.bfloat16
HIGHEST = lax.Precision.HIGHEST

D_MODEL = 1024
POOL_WINDOWS = (2, 4, 8, 16)
POOL_WIDTH = 512
POOL_GROUP_DIM = 128
POOL_HALO = 16
ATTN_HEADS = 8
HEAD_DIM = 64
ATTN_WIDTH = 512
MOBA_BLOCK = 256
MOBA_TOPK = 3
ROPE_THETA = 500000.0
ROT_HALF = 8
PEER_HEADS = 8
PEER_NKEYS = 128
PEER_EXPERTS = PEER_NKEYS * PEER_NKEYS
PEER_HALF = 128
PEER_TOPK = 16
EPS = 1e-6

LANES = 128
SUBLANES = 8
MXU_DEPTH = 256
MATMUL_ROW_CHUNKS = 4
HEAD_PAD = LANES
ATTN_PAD = ATTN_HEADS * HEAD_PAD
BIAS_LANE0 = HEAD_DIM
ONES_LANE = HEAD_DIM
MASK_BIG = 2.0 ** 100
CAUSAL_NEG = -1e30
NOT_MEMBER = 255.0
SQRT_HALF = float(np.sqrt(0.5).astype(np.float32))
VMEM_LIMIT = 56 * 1024 * 1024

IN_PROJ_PAD = POOL_WIDTH + 3 * ATTN_PAD + 2 * D_MODEL
Q_OFF = POOL_WIDTH
K_OFF = Q_OFF + ATTN_PAD
V_OFF = K_OFF + ATTN_PAD
G_OFF = V_OFF + ATTN_PAD


def _params(*semantics):
    return pltpu.CompilerParams(dimension_semantics=semantics, vmem_limit_bytes=VMEM_LIMIT)


def _const_spec(shape):
    nd = len(shape)
    return pl.BlockSpec(shape, lambda *_: (0,) * nd, pipeline_mode=pl.Buffered(1))


def _rms_modulate(x, g, shift, scale):
    ms = jnp.mean(x * x, axis=-1, keepdims=True)
    return (x * lax.rsqrt(ms + EPS) * g) * (1.0 + scale) + shift


def _ada_kernel(c_ref, w_ref, b_ref, o_ref):
    c = c_ref[...]
    act = c / (1.0 + jnp.exp(-c))
    o_ref[...] = jnp.dot(act, w_ref[...], precision=HIGHEST,
                         preferred_element_type=F32) + b_ref[...]


def _ada(c8, w, b):
    n = w.shape[1]
    tn = 1536
    return pl.pallas_call(
        _ada_kernel,
        grid=(n // tn,),
        in_specs=[pl.BlockSpec((8, D_MODEL), lambda j: (0, 0)),
                  pl.BlockSpec((D_MODEL, tn), lambda j: (0, j)),
                  pl.BlockSpec((1, tn), lambda j: (0, j))],
        out_specs=pl.BlockSpec((8, tn), lambda j: (0, j)),
        out_shape=jax.ShapeDtypeStruct((8, n), F32),
        compiler_params=_params("arbitrary"),
        name="ada_mod",
    )(c8, w, b)


def _inproj_kernel(x_ref, sh_ref, sc_ref, g_ref, w_ref, cos_ref, sa_ref, sb_ref,
                   u_ref, q_ref, k_ref, v_ref, gt_ref):
    h = _rms_modulate(x_ref[...], g_ref[...], sh_ref[0], sc_ref[0]).astype(BF16)

    def proj(off, width):
        return jnp.dot(h, w_ref[:, off:off + width], preferred_element_type=F32)

    u_ref[...] = proj(0, POOL_WIDTH)
    cos, sa, sb = cos_ref[...], sa_ref[...], sb_ref[...]

    def rope(t):
        return t * cos + pltpu.roll(t, LANES - ROT_HALF, 1) * sa + pltpu.roll(t, ROT_HALF, 1) * sb

    for hd in range(ATTN_HEADS):
        sl = slice(hd * HEAD_PAD, (hd + 1) * HEAD_PAD)
        q = rope(proj(Q_OFF + hd * HEAD_PAD, HEAD_PAD))
        q_ref[:, sl] = (q * (HEAD_DIM ** -0.5)).astype(BF16)
        k_ref[:, sl] = rope(proj(K_OFF + hd * HEAD_PAD, HEAD_PAD)).astype(BF16)
    v_ref[...] = proj(V_OFF, ATTN_PAD).astype(BF16)
    gt_ref[...] = 1.0 / (1.0 + jnp.exp(-proj(G_OFF, 2 * D_MODEL)))


def _inproj(x2, shift, scale, g, w_p, cos_t, sa_t, sb_t, *, seq, tm):
    t = x2.shape[0]
    tpb = seq // tm
    row = lambda i: (i, 0)
    per_batch = lambda i: (i // tpb, 0, 0)
    per_pos = lambda i: (i % tpb, 0)
    return pl.pallas_call(
        _inproj_kernel,
        grid=(t // tm,),
        in_specs=[pl.BlockSpec((tm, D_MODEL), row),
                  pl.BlockSpec((1, 1, D_MODEL), per_batch),
                  pl.BlockSpec((1, 1, D_MODEL), per_batch),
                  _const_spec((1, D_MODEL)),
                  _const_spec((D_MODEL, IN_PROJ_PAD)),
                  pl.BlockSpec((tm, LANES), per_pos),
                  pl.BlockSpec((tm, LANES), per_pos),
                  pl.BlockSpec((tm, LANES), per_pos)],
        out_specs=[pl.BlockSpec((tm, POOL_WIDTH), row),
                   pl.BlockSpec((tm, ATTN_PAD), row),
                   pl.BlockSpec((tm, ATTN_PAD), row),
                   pl.BlockSpec((tm, ATTN_PAD), row),
                   pl.BlockSpec((tm, 2 * D_MODEL), row)],
        out_shape=[jax.ShapeDtypeStruct((t, POOL_WIDTH), F32),
                   jax.ShapeDtypeStruct((t, ATTN_PAD), BF16),
                   jax.ShapeDtypeStruct((t, ATTN_PAD), BF16),
                   jax.ShapeDtypeStruct((t, ATTN_PAD), BF16),
                   jax.ShapeDtypeStruct((t, 2 * D_MODEL), F32)],
        compiler_params=_params("arbitrary"),
        name="in_proj",
    )(x2, shift, scale, g, w_p, cos_t, sa_t, sb_t)


def _moba_kernel(q_ref, k_ref, v_ref, bsel_ref, o_ref, qa_scr, ka_scr, va_scr, *, n_blocks,
                 group, heads):
    qi = pl.program_id(2)
    blk = MOBA_BLOCK
    lane = lax.broadcasted_iota(jnp.int32, (blk, LANES), 1)
    lane_f = lane.astype(F32)
    head_lanes = [slice(i * HEAD_PAD, (i + 1) * HEAD_PAD) for i in range(heads)]
    nt = (((1,), (1,)), ((), ()))

    @pl.when(qi == 0)
    def _prepare():
        kms = [jnp.dot(bsel_ref[...], k_ref[:, hl], preferred_element_type=F32)
               for hl in head_lanes]

        def per_block(nb, carry):
            rows = pl.ds(pl.multiple_of(nb * blk, blk), blk)
            valid = (lane >= BIAS_LANE0) & (lane < BIAS_LANE0 + nb)
            bias_lanes = (lane >= BIAS_LANE0) & (lane < BIAS_LANE0 + n_blocks)
            onehot = jnp.where(lane == BIAS_LANE0 + nb, 1.0, 0.0)
            qs = [q_ref[rows, hl].astype(F32) for hl in head_lanes]
            works = [jnp.where(valid, lax.dot_general(q, km, nt, precision=HIGHEST,
                                                      preferred_element_type=F32), -jnp.inf)
                     for q, km in zip(qs, kms)]
            chosen = [jnp.zeros((blk, LANES), F32) for _ in head_lanes]
            for _ in range(MOBA_TOPK):
                ms = [jnp.max(w, axis=1, keepdims=True) for w in works]
                idxs = [jnp.min(jnp.where(w == m, lane_f, 2.0 * LANES), axis=1, keepdims=True)
                        for w, m in zip(works, ms)]
                hits = [(lane_f == idx) & valid for idx in idxs]
                chosen = [jnp.where(hit, 1.0, ch) for hit, ch in zip(hits, chosen)]
                works = [jnp.where(hit, -jnp.inf, w) for hit, w in zip(hits, works)]
            for hl, q, ch in zip(head_lanes, qs, chosen):
                masked = bias_lanes & (ch == 0.0)
                qa_scr[rows, hl] = jnp.where(masked, -MASK_BIG, q).astype(BF16)
                ka_scr[rows, hl] = jnp.where(lane < HEAD_DIM, k_ref[rows, hl].astype(F32),
                                             onehot).astype(BF16)
                va_scr[rows, hl] = jnp.where(lane == ONES_LANE, 1.0,
                                             v_ref[rows, hl].astype(F32)).astype(BF16)
            return carry

        lax.fori_loop(0, n_blocks, per_block, 0)

    own = pl.ds(pl.multiple_of(qi * blk, blk), blk)
    r_i = lax.broadcasted_iota(jnp.int32, (blk, blk), 0)
    c_i = lax.broadcasted_iota(jnp.int32, (blk, blk), 1)
    qas = [qa_scr[own, hl] for hl in head_lanes]
    own_s = [jnp.where(c_i <= r_i,
                       lax.dot_general(jnp.where(lane < HEAD_DIM, qa, jnp.zeros_like(qa)),
                                       ka_scr[own, hl], nt, preferred_element_type=F32),
                       CAUSAL_NEG) for hl, qa in zip(head_lanes, qas)]
    own_m = [jnp.max(s, axis=1, keepdims=True) for s in own_s]
    init = [(m0, jnp.dot(jnp.exp(s - m0).astype(BF16), va_scr[own, hl],
                         preferred_element_type=F32))
            for hl, s, m0 in zip(head_lanes, own_s, own_m)]

    span = group * blk

    def past(j, carry):
        rows = pl.ds(pl.multiple_of(j * span, span), span)
        scores = [lax.dot_general(qa, ka_scr[rows, hl], nt, preferred_element_type=F32)
                  for hl, qa in zip(head_lanes, qas)]
        m_new = [jnp.maximum(m, jnp.max(sj, axis=1, keepdims=True))
                 for sj, (m, _) in zip(scores, carry)]
        out = []
        for hl, sj, mn, (m, acc) in zip(head_lanes, scores, m_new, carry):
            pj = jnp.exp(sj - mn)
            acc = acc * jnp.exp(m - mn) + jnp.dot(pj.astype(BF16), va_scr[rows, hl],
                                                  preferred_element_type=F32)
            out.append((mn, acc))
        return tuple(out)

    final = lax.fori_loop(0, (qi + group - 1) // group, past, tuple(init))
    for hl, (_, acc) in zip(head_lanes, final):
        o_ref[:, hl] = (acc / acc[:, ONES_LANE:ONES_LANE + 1]).astype(BF16)


def _moba(qp, kp, vp, bsel, *, batch, seq, group, heads):
    n_blocks = seq // MOBA_BLOCK
    assert n_blocks % group == 0 and ATTN_HEADS % heads == 0
    width = heads * HEAD_PAD
    head = lambda b, h, i: (b, h)
    return pl.pallas_call(
        functools.partial(_moba_kernel, n_blocks=n_blocks, group=group, heads=heads),
        grid=(batch, ATTN_HEADS // heads, n_blocks),
        in_specs=[pl.BlockSpec((seq, width), head),
                  pl.BlockSpec((seq, width), head),
                  pl.BlockSpec((seq, width), head),
                  _const_spec((LANES, seq))],
        out_specs=pl.BlockSpec((MOBA_BLOCK, width), lambda b, h, i: (b * n_blocks + i, h)),
        out_shape=jax.ShapeDtypeStruct((batch * seq, ATTN_PAD), BF16),
        scratch_shapes=[pltpu.VMEM((seq, width), BF16)] * 3,
        compiler_params=_params("arbitrary", "arbitrary", "arbitrary"),
        name="moba_attn",
    )(qp, kp, vp, bsel)


def _merge_kernel(u_ref, uprev_ref, attn_ref, gt_ref, x_ref, poolw_ref, pscale_ref, wp_ref,
                  wa_ref, wo_ref, gate1_ref, g2_ref, sh2_ref, sc2_ref, x1_ref, h2t_ref,
                  *, tm, tpb):
    i = pl.program_id(0)
    first = (i % tpb) == 0
    pos = lax.broadcasted_iota(jnp.int32, (tm, LANES), 0) + (i % tpb) * tm
    mixed = []
    for g, w in enumerate(POOL_WINDOWS):
        sl = slice(g * POOL_GROUP_DIM, (g + 1) * POOL_GROUP_DIM)
        u = u_ref[:, sl]
        halo = jnp.where(first, 0.0, uprev_ref[:, sl])
        ext = jnp.concatenate([halo, u], axis=0)
        span = 1
        while span < w:
            ext = ext + pltpu.roll(ext, span, 0)
            span *= 2
        cnt = jnp.minimum(pos + 1, w).astype(F32)
        diff = ext[POOL_HALO:, :] / cnt - u
        m = jnp.dot(diff.astype(BF16), poolw_ref[g], preferred_element_type=F32)
        mixed.append((m * pscale_ref[:, sl]).astype(BF16))
    pooled = jnp.concatenate(mixed, axis=1)
    bp = jnp.dot(pooled, wp_ref[...], preferred_element_type=F32)
    ba = jnp.dot(attn_ref[...], wa_ref[...], preferred_element_type=F32)
    merged = gt_ref[:, :D_MODEL] * bp + gt_ref[:, D_MODEL:] * ba
    y = jnp.dot(merged.astype(BF16), wo_ref[...], preferred_element_type=F32)
    x1 = x_ref[...] + gate1_ref[0] * y
    x1_ref[...] = x1
    h2 = _rms_modulate(x1, g2_ref[...], sh2_ref[0], sc2_ref[0])
    h2t_ref[...] = h2.T.astype(BF16)


def _merge(u_pool, attn, gates, x2, poolw, pscale, wp, wa, wo, gate1, g2, shift2, scale2,
           *, seq, tm):
    t = x2.shape[0]
    tpb = seq // tm
    row = lambda i: (i, 0)
    per_batch = lambda i: (i // tpb, 0, 0)
    halo_blocks = tm // POOL_HALO
    return pl.pallas_call(
        functools.partial(_merge_kernel, tm=tm, tpb=tpb),
        grid=(t // tm,),
        in_specs=[pl.BlockSpec((tm, POOL_WIDTH), row),
                  pl.BlockSpec((POOL_HALO, POOL_WIDTH),
                               lambda i: (jnp.maximum(i * halo_blocks - 1, 0), 0)),
                  pl.BlockSpec((tm, ATTN_PAD), row),
                  pl.BlockSpec((tm, 2 * D_MODEL), row),
                  pl.BlockSpec((tm, D_MODEL), row),
                  _const_spec((len(POOL_WINDOWS), POOL_GROUP_DIM, POOL_GROUP_DIM)),
                  _const_spec((1, POOL_WIDTH)),
                  _const_spec((POOL_WIDTH, D_MODEL)),
                  _const_spec((ATTN_PAD, D_MODEL)),
                  _const_spec((D_MODEL, D_MODEL)),
                  pl.BlockSpec((1, 1, D_MODEL), per_batch),
                  _const_spec((1, D_MODEL)),
                  pl.BlockSpec((1, 1, D_MODEL), per_batch),
                  pl.BlockSpec((1, 1, D_MODEL), per_batch)],
        out_specs=[pl.BlockSpec((tm, D_MODEL), row),
                   pl.BlockSpec((D_MODEL, tm), lambda i: (0, i))],
        out_shape=[jax.ShapeDtypeStruct((t, D_MODEL), F32),
                   jax.ShapeDtypeStruct((D_MODEL, t), BF16)],
        compiler_params=_params("arbitrary"),
        name="mixer_merge",
    )(u_pool, u_pool, attn, gates, x2, poolw, pscale, wp, wa, wo, gate1, g2, shift2, scale2)


def _extract_top16(streams):
    width = streams[0][0].shape[1]
    slot = lax.broadcasted_iota(jnp.int32, (PEER_TOPK, width), 0)

    def body(r, carry):
        out = []
        for (work, rank, vals), (_, index_f) in zip(carry, streams):
            m = jnp.max(work, axis=0, keepdims=True)
            idx = jnp.min(jnp.where(work == m, index_f, jnp.inf), axis=0, keepdims=True)
            hit = index_f == idx
            out.append((jnp.where(hit, -jnp.inf, work),
                        jnp.where(hit, lax.convert_element_type(r, F32), rank),
                        jnp.where(slot == r, m, vals)))
        return tuple(out)

    init = tuple((sc, jnp.full(sc.shape, NOT_MEMBER, F32), jnp.zeros((PEER_TOPK, width), F32))
                 for sc, _ in streams)
    return [(vals, rank) for _, rank, vals in lax.fori_loop(0, PEER_TOPK, body, init)]


def _batcher_pairs(n):
    pairs, p = [], 1
    while p < n:
        k = p
        while k >= 1:
            for j in range(k % p, n - k, 2 * k):
                for i in range(min(k, n - j - k)):
                    if (i + j) // (2 * p) == (i + j + k) // (2 * p):
                        pairs.append((i + j, i + j + k))
            k //= 2
        p *= 2
    return pairs


def _sorted_top16(regs):
    x = list(regs)

    def exchange(i, j):
        hi, lo = x[i], x[j]
        if lo is None:
            return
        if hi is None:
            x[i], x[j] = lo, None
        else:
            x[i], x[j] = jnp.maximum(hi, lo), jnp.minimum(hi, lo)

    for i, j in _batcher_pairs(PEER_TOPK):
        exchange(i, j)
    for shift in (4, 2, 1):
        other = [None if v is None else pltpu.roll(v, shift, 0) for v in x]
        for i in range(PEER_TOPK):
            mine, theirs = x[i], other[PEER_TOPK - 1 - i]
            x[i] = theirs if mine is None else mine if theirs is None else jnp.maximum(mine, theirs)
        dist = PEER_TOPK // 2
        while dist >= 1:
            for i in range(PEER_TOPK):
                if i & dist == 0:
                    exchange(i, i + dist)
            dist //= 2
    return x


def _route_kernel(h2t_ref, wqt_ref, keys_ref, r2_ref, e1_ref, n_ref, c_ref, qt_scr, *, tm):
    qt_scr[...] = jnp.dot(wqt_ref[...], h2t_ref[...], preferred_element_type=F32)
    half = PEER_TOPK // 2
    width = LANES

    def iota_f(rows):
        return lax.broadcasted_iota(jnp.int32, (rows, width), 0).astype(F32)

    tail = PEER_TOPK + (half - 1) * half
    slot = lax.broadcasted_iota(jnp.int32, (half, width), 0)

    def candidates(a, b):
        return jnp.concatenate([a[0:1, :] + b]
                               + [a[r:r + 1, :] + b[:half, :] for r in range(1, half)]
                               + [a[half:, :] + b[0:1, :]], axis=0)

    def staircase(chosen, a, b):
        e0 = jnp.exp(a - a[0:1, :])
        e1 = jnp.exp(b - b[0:1, :])
        n_lo = jnp.zeros((half, width), F32)
        pref_lo = jnp.zeros((half, width), F32)
        for r in range(half):
            lo = 0 if r == 0 else PEER_TOPK + (r - 1) * half
            cnt = PEER_TOPK if r == 0 else half
            grp = chosen[lo:lo + cnt, :]
            n_lo = jnp.where(slot == r, jnp.sum(grp, axis=0, keepdims=True), n_lo)
            pref_lo = jnp.where(slot == r, jnp.sum(grp * e1[:cnt, :], axis=0, keepdims=True),
                                pref_lo)
        n = jnp.concatenate([n_lo, chosen[tail:, :]], axis=0)
        pref = jnp.concatenate([pref_lo, chosen[tail:, :]], axis=0)
        return n, jnp.sum(e0 * pref, axis=0, keepdims=True)

    def emit(h, cols, s0, s1, a, b, rank2, n_dense, z):
        r2_ref[h, :, cols] = rank2
        e1_ref[h, :, cols] = jnp.exp(s1 - b[0:1, :])
        n_ref[h, :, cols] = n_dense
        c_ref[h, :, cols] = 0.5 * jnp.exp(s0 - a[0:1, :]) / z

    def per_head(h, carry):
        for cc in range(tm // width):
            route(h, slice(cc * width, (cc + 1) * width))
        return carry

    def route(h, cols):
        def half_scores(p):
            rows = pl.ds(pl.multiple_of((2 * h + p) * PEER_HALF, PEER_HALF), PEER_HALF)
            return jnp.dot(keys_ref[2 * h + p], qt_scr[rows, cols], precision=HIGHEST,
                           preferred_element_type=F32)

        s0 = half_scores(0)
        s1 = half_scores(1)

        pieces = PEER_NKEYS // SUBLANES
        s0v = [s0[v * SUBLANES:(v + 1) * SUBLANES, :] for v in range(pieces)]
        s1v = [s1[v * SUBLANES:(v + 1) * SUBLANES, :] for v in range(pieces)]
        a = _sorted_top16(s0v)
        b = _sorted_top16(s1v)
        sub = lax.broadcasted_iota(jnp.int32, (SUBLANES, width), 0)

        def stack(vals):
            out = vals[0]
            for j in range(1, SUBLANES):
                out = jnp.where(sub == j, vals[j], out)
            return out

        def total(x):
            return jnp.broadcast_to(jnp.sum(x, axis=0, keepdims=True), (SUBLANES, width))

        b_lo, b_hi, a_hi = stack(b[:half]), stack(b[half:]), stack(a[half:])
        cand = ([a[0] + b_lo, a[0] + b_hi] + [a[r] + b_lo for r in range(1, half)]
                + [a_hi + b[0]])
        top = _sorted_top16(cand + [None] * (PEER_TOPK - len(cand)))
        chosen = [jnp.where(c >= top[-1], 1.0, 0.0) for c in cand]
        n = ([total(chosen[0] + chosen[1])] + [total(chosen[1 + r]) for r in range(1, half)]
             + [total(jnp.where(sub == j, chosen[-1], 0.0)) for j in range(half)])
        z = total(sum(ch * jnp.exp(c - top[0]) for ch, c in zip(chosen, cand)))
        rank2, n_dense = [], []
        for v in range(pieces):
            r2, nd = jnp.full((SUBLANES, width), NOT_MEMBER, F32), jnp.zeros((SUBLANES, width), F32)
            for r in range(PEER_TOPK):
                r2 = jnp.where(s1v[v] == b[r], float(r), r2)
                nd = jnp.where(s0v[v] == a[r], n[r], nd)
            rank2.append(r2)
            n_dense.append(nd)
        emit(h, cols, s0, s1, a[0], b[0], jnp.concatenate(rank2, axis=0),
             jnp.concatenate(n_dense, axis=0), z[0:1, :])

        distinct = jnp.ones((SUBLANES, width), F32)
        for lst in (a, b, top):
            for r in range(PEER_TOPK - 1):
                distinct = jnp.where(lst[r] == lst[r + 1], 0.0, distinct)
        at_least = lambda regs, thr: total(sum(jnp.where(x >= thr, 1.0, 0.0) for x in regs))
        sixteen = ((at_least(s0v, a[-1]) == PEER_TOPK) & (at_least(s1v, b[-1]) == PEER_TOPK)
                   & (total(sum(chosen)) == PEER_TOPK))
        tie_free = jnp.min(jnp.where(sixteen, distinct, 0.0))

        @pl.when(tie_free < 0.5)
        def _exact():
            key_f = iota_f(PEER_NKEYS)
            pair_f = jnp.concatenate(
                [iota_f(PEER_TOPK)]
                + [iota_f(half) + float(r * PEER_TOPK) for r in range(1, half)]
                + [(iota_f(half) + float(half)) * float(PEER_TOPK)], axis=0)
            (a, rank1), (b, rank2) = _extract_top16([(s0, key_f), (s1, key_f)])
            ((_, pick),) = _extract_top16([(candidates(a, b), pair_f)])
            n, z = staircase(jnp.where(pick < NOT_MEMBER, 1.0, 0.0), a, b)
            n_dense = jnp.zeros((PEER_NKEYS, width), F32)
            for r in range(PEER_TOPK):
                n_dense = jnp.where(rank1 == float(r), n[r:r + 1, :], n_dense)
            emit(h, cols, s0, s1, a, b, rank2, n_dense, z)

    lax.fori_loop(0, PEER_HEADS, per_head, 0)


def _route(h2t, wqt, keys, *, tm):
    t = h2t.shape[1]
    dense = jax.ShapeDtypeStruct((PEER_HEADS, PEER_NKEYS, t), F32)
    dense_spec = pl.BlockSpec((PEER_HEADS, PEER_NKEYS, tm), lambda i: (0, 0, i))
    return pl.pallas_call(
        functools.partial(_route_kernel, tm=tm),
        grid=(t // tm,),
        in_specs=[pl.BlockSpec((D_MODEL, tm), lambda i: (0, i)),
                  _const_spec((2 * PEER_HEADS * PEER_HALF, D_MODEL)),
                  _const_spec((2 * PEER_HEADS, PEER_NKEYS, PEER_HALF))],
        out_specs=[dense_spec] * 4,
        out_shape=[dense] * 4,
        scratch_shapes=[pltpu.VMEM((2 * PEER_HEADS * PEER_HALF, tm), F32)],
        compiler_params=_params("arbitrary"),
        name="peer_route",
    )(h2t, wqt, keys)


def _experts_kernel(h2t_ref, u_ref, vt_ref, r2_ref, e1_ref, n_ref, c_ref, x1_ref, gate2_ref,
                    gf_ref, o_ref, acc_scr, act_a, act_b, p_a, p_b, *, tm, sub_blocks):
    s = pl.program_id(1)
    last = pl.num_programs(1) - 1

    @pl.when(s == 0)
    def _fill():
        acc_scr[...] = jnp.zeros_like(acc_scr)
        p_b[...] = jnp.zeros_like(p_b)
        act_a[...] = jnp.dot(u_ref[...], h2t_ref[...], preferred_element_type=F32)

    quad = 4 * SUBLANES

    def step(act_new, act_old, p_new, p_old):
        n_cc = tm // LANES
        n_kq = PEER_NKEYS // quad
        eb = act_new.shape[0]
        k_split = D_MODEL // MXU_DEPTH
        m_split = MATMUL_ROW_CHUNKS
        m1, k1 = eb // m_split, D_MODEL // k_split
        m2, k2 = D_MODEL // m_split, eb // k_split
        stride = 2 * n_cc * n_kq // (m_split * k_split)

        def matmul_pieces(slot):
            if slot % stride:
                return
            piece = slot // stride
            mi, ki = divmod(piece, k_split)
            u_rows = slice(mi * m1, (mi + 1) * m1)
            ks = slice(ki * k1, (ki + 1) * k1)
            part = jnp.dot(u_ref[u_rows, ks], h2t_ref[ks, :], preferred_element_type=F32)
            if ki == 0:
                act_new[u_rows, :] = part
            else:
                act_new[u_rows, :] += part
            d_rows = slice(mi * m2, (mi + 1) * m2)
            ks = slice(ki * k2, (ki + 1) * k2)
            acc_scr[d_rows, :] += jnp.dot(vt_ref[d_rows, ks], p_old[ks, :],
                                          preferred_element_type=F32)

        for cc in range(n_cc):
            cols = slice(cc * LANES, (cc + 1) * LANES)
            for kq in range(n_kq):
                block = cc * n_kq + kq
                matmul_pieces(2 * block)
                i2 = slice(kq * quad, (kq + 1) * quad)
                w = [jnp.zeros((quad, LANES), F32) for _ in range(sub_blocks)]
                for h in range(PEER_HEADS):
                    r2 = r2_ref[h, i2, cols]
                    e1 = e1_ref[h, i2, cols]
                    for jb in range(sub_blocks):
                        keep = r2 < n_ref[h, jb:jb + 1, cols]
                        w[jb] = w[jb] + jnp.where(keep, e1 * c_ref[h, jb:jb + 1, cols], 0.0)
                matmul_pieces(2 * block + 1)
                for jb in range(sub_blocks):
                    rows = slice(jb * PEER_NKEYS + kq * quad, jb * PEER_NKEYS + (kq + 1) * quad)
                    a = act_old[rows, cols]
                    p_new[rows, cols] = (w[jb] * (a * (1.0 + lax.erf(a * SQRT_HALF)))).astype(BF16)

    @pl.when((s % 2 == 0) & (s > 0))
    def _even():
        step(act_a, act_b, p_b, p_a)

    @pl.when((s % 2 == 1) & (s < last))
    def _odd():
        step(act_b, act_a, p_a, p_b)

    @pl.when(s == last)
    def _drain():
        acc_scr[...] += jnp.dot(vt_ref[...], p_b[...], preferred_element_type=F32)
        x2 = x1_ref[...] + gate2_ref[0] * acc_scr[...].T
        ms = jnp.mean(x2 * x2, axis=-1, keepdims=True)
        o_ref[...] = x2 * lax.rsqrt(ms + EPS) * gf_ref[...]


def _experts(h2t, u, v, r2, e1, n, c, x1, gate2, gf, *, seq, tm, sub_blocks):
    t = h2t.shape[1]
    tpb = seq // tm
    eb = sub_blocks * PEER_NKEYS
    n_eb = PEER_EXPERTS // eb
    assert n_eb % 2 == 0
    u_b = u.astype(BF16)
    vt_b = v.reshape(n_eb, eb, D_MODEL).transpose(0, 2, 1).astype(BF16)
    tok3 = lambda i, s: (0, 0, i)
    blk = lambda s, lag: jnp.clip(s - lag, 0, n_eb - 1)
    return pl.pallas_call(
        functools.partial(_experts_kernel, tm=tm, sub_blocks=sub_blocks),
        grid=(t // tm, n_eb + 2),
        in_specs=[pl.BlockSpec((D_MODEL, tm), lambda i, s: (0, i)),
                  pl.BlockSpec((eb, D_MODEL), lambda i, s: (blk(s, 0), 0)),
                  pl.BlockSpec((None, D_MODEL, eb), lambda i, s: (blk(s, 2), 0, 0)),
                  pl.BlockSpec((PEER_HEADS, PEER_NKEYS, tm), tok3),
                  pl.BlockSpec((PEER_HEADS, PEER_NKEYS, tm), tok3),
                  pl.BlockSpec((PEER_HEADS, sub_blocks, tm), lambda i, s: (0, blk(s, 1), i)),
                  pl.BlockSpec((PEER_HEADS, sub_blocks, tm), lambda i, s: (0, blk(s, 1), i)),
                  pl.BlockSpec((tm, D_MODEL), lambda i, s: (i, 0)),
                  pl.BlockSpec((1, 1, D_MODEL), lambda i, s: (i // tpb, 0, 0)),
                  pl.BlockSpec((1, D_MODEL), lambda i, s: (0, 0))],
        out_specs=pl.BlockSpec((tm, D_MODEL), lambda i, s: (i, 0)),
        out_shape=jax.ShapeDtypeStruct((t, D_MODEL), F32),
        scratch_shapes=[pltpu.VMEM((D_MODEL, tm), F32),
                        pltpu.VMEM((eb, tm), F32), pltpu.VMEM((eb, tm), F32),
                        pltpu.VMEM((eb, tm), BF16), pltpu.VMEM((eb, tm), BF16)],
        compiler_params=_params("arbitrary", "arbitrary"),
        name="peer_experts",
    )(h2t, u_b, vt_b, r2, e1, n, c, x1, gate2, gf)


def _pad_heads(w):
    d = w.shape[0]
    w = w.reshape(d, ATTN_HEADS, HEAD_DIM)
    return jnp.pad(w, ((0, 0), (0, 0), (0, HEAD_PAD - HEAD_DIM))).reshape(d, ATTN_PAD)


def _rope_tables(seq):
    inv = ROPE_THETA ** (-jnp.arange(ROT_HALF, dtype=F32) / ROT_HALF)
    ang = jnp.arange(seq).astype(F32)[:, None] * inv[None, :]
    cos, sin = jnp.cos(ang), jnp.sin(ang)
    z = lambda n: jnp.zeros((seq, n), F32)
    cos_t = jnp.concatenate([cos, cos, jnp.ones((seq, LANES - 2 * ROT_HALF), F32)], axis=1)
    sa_t = jnp.concatenate([-sin, z(LANES - ROT_HALF)], axis=1)
    sb_t = jnp.concatenate([z(ROT_HALF), sin, z(LANES - 2 * ROT_HALF)], axis=1)
    return cos_t, sa_t, sb_t


def _block_mean_rows(seq):
    r = jnp.arange(LANES)[:, None]
    s = jnp.arange(seq)[None, :]
    return jnp.where(r - BIAS_LANE0 == s // MOBA_BLOCK, 1.0 / MOBA_BLOCK, 0.0).astype(BF16)


def kernel(x, c, w_ada, b_ada, norm_mix_g, w_in, pool_w, pool_scale, w_branch_pool,
           w_branch_attn, w_out, norm_ffn_g, peer_wq, peer_sub_keys, peer_u, peer_v,
           norm_final_g):
    batch, seq, d = x.shape
    depth = w_ada.shape[0]
    assert d == D_MODEL and batch <= 8 and seq % 512 == 0
    assert seq // MOBA_BLOCK <= LANES - BIAS_LANE0
    assert depth == 1
    t = batch * seq
    x2 = x.reshape(t, d)
    c8 = jnp.pad(c, ((0, 8 - batch), (0, 0)))
    cos_t, sa_t, sb_t = _rope_tables(seq)
    bsel = _block_mean_rows(seq)
    row1 = lambda v: v.reshape(1, -1)
    per_batch = lambda v: v.reshape(batch, 1, d)

    for l in range(depth):
        mod = _ada(c8, w_ada[l], row1(b_ada[l]))[:batch]
        shift1, scale1, gate1, shift2, scale2, gate2 = [per_batch(m) for m in
                                                        jnp.split(mod, 6, axis=-1)]
        wl = w_in[l]
        w_p = jnp.concatenate(
            [wl[:, :POOL_WIDTH],
             _pad_heads(wl[:, POOL_WIDTH:POOL_WIDTH + ATTN_WIDTH]),
             _pad_heads(wl[:, POOL_WIDTH + ATTN_WIDTH:POOL_WIDTH + 2 * ATTN_WIDTH]),
             _pad_heads(wl[:, POOL_WIDTH + 2 * ATTN_WIDTH:POOL_WIDTH + 3 * ATTN_WIDTH]),
             wl[:, POOL_WIDTH + 3 * ATTN_WIDTH:]], axis=1).astype(BF16)
        u_pool, qp, kp, vp, gates = _inproj(x2, shift1, scale1, row1(norm_mix_g[l]), w_p,
                                            cos_t, sa_t, sb_t, seq=seq, tm=256)
        attn = _moba(qp, kp, vp, bsel, batch=batch, seq=seq, group=4, heads=2)
        wa_p = _pad_heads(w_branch_attn[l].T).T.astype(BF16)
        x1, h2t = _merge(u_pool, attn, gates, x2, pool_w[l].astype(BF16), row1(pool_scale[l]),
                         w_branch_pool[l].astype(BF16), wa_p, w_out[l].astype(BF16), gate1,
                         row1(norm_ffn_g[l]), shift2, scale2, seq=seq, tm=256)
        keys = peer_sub_keys[l].reshape(2 * PEER_HEADS, PEER_NKEYS, PEER_HALF)
        r2, e1, n, cden = _route(h2t, peer_wq[l].T.astype(BF16), keys, tm=256)
        x2 = _experts(h2t, peer_u[l], peer_v[l], r2, e1, n, cden,
                      x1, gate2, row1(norm_final_g), seq=seq, tm=512, sub_blocks=8)
    return x2.reshape(batch, seq, d)
```

```python
import functools
import math

import jax
import jax.numpy as jnp
import numpy as np
from jax import lax
from jax.experimental import pallas as pl
from jax.experimental.pallas import tpu as pltpu

F32 = jnp.float32
BF16 = jnp.bfloat16
HIGHEST = lax.Precision.HIGHEST

D_MODEL = 1024
POOL_WINDOWS = (2, 4, 8, 16)
POOL_WIDTH = 512
POOL_GROUP_DIM = 128
POOL_HALO = 16
ATTN_HEADS = 8
HEAD_DIM = 64
ATTN_WIDTH = 512
MOBA_BLOCK = 256
MOBA_TOPK = 3
ROPE_THETA = 500000.0
ROT_HALF = 8
PEER_HEADS = 8
PEER_NKEYS = 128
PEER_EXPERTS = PEER_NKEYS * PEER_NKEYS
PEER_HALF = 128
PEER_TOPK = 16
EPS = 1e-6

LANES = 128
SUBLANES = 8
MXU_DEPTH = 256
MATMUL_ROW_CHUNKS = 4
HEAD_PAD = LANES
ATTN_PAD = ATTN_HEADS * HEAD_PAD
BIAS_LANE0 = HEAD_DIM
ONES_LANE = HEAD_DIM
MASK_BIG = 2.0 ** 100
CAUSAL_NEG = -1e30
NOT_MEMBER = 255.0
SQRT_HALF = float(np.sqrt(0.5).astype(np.float32))
VMEM_LIMIT = 56 * 1024 * 1024

IN_PROJ_PAD = POOL_WIDTH + 3 * ATTN_PAD + 2 * D_MODEL
Q_OFF = POOL_WIDTH
K_OFF = Q_OFF + ATTN_PAD
V_OFF = K_OFF + ATTN_PAD
G_OFF = V_OFF + ATTN_PAD


def _params(*semantics):
    return pltpu.CompilerParams(dimension_semantics=semantics, vmem_limit_bytes=VMEM_LIMIT)


def _const_spec(shape):
    nd = len(shape)
    return pl.BlockSpec(shape, lambda *_: (0,) * nd, pipeline_mode=pl.Buffered(1))


def _rms_modulate(x, g, shift, scale):
    ms = jnp.mean(x * x, axis=-1, keepdims=True)
    return (x * lax.rsqrt(ms + EPS) * g) * (1.0 + scale) + shift


def _ada_kernel(c_ref, w_ref, b_ref, o_ref):
    c = c_ref[...]
    act = c / (1.0 + jnp.exp(-c))
    o_ref[...] = jnp.dot(act, w_ref[...], precision=HIGHEST,
                         preferred_element_type=F32) + b_ref[...]


def _ada(c8, w, b):
    n = w.shape[1]
    tn = 1536
    return pl.pallas_call(
        _ada_kernel,
        grid=(n // tn,),
        in_specs=[pl.BlockSpec((8, D_MODEL), lambda j: (0, 0)),
                  pl.BlockSpec((D_MODEL, tn), lambda j: (0, j)),
                  pl.BlockSpec((1, tn), lambda j: (0, j))],
        out_specs=pl.BlockSpec((8, tn), lambda j: (0, j)),
        out_shape=jax.ShapeDtypeStruct((8, n), F32),
        compiler_params=_params("arbitrary"),
        name="ada_mod",
    )(c8, w, b)


def _inproj_kernel(x_ref, sh_ref, sc_ref, g_ref, w_ref, cos_ref, sa_ref, sb_ref,
                   u_ref, q_ref, k_ref, v_ref, gt_ref):
    h = _rms_modulate(x_ref[...], g_ref[...], sh_ref[0], sc_ref[0]).astype(BF16)

    def proj(off, width):
        return jnp.dot(h, w_ref[:, off:off + width], preferred_element_type=F32)

    u_ref[...] = proj(0, POOL_WIDTH)
    cos, sa, sb = cos_ref[...], sa_ref[...], sb_ref[...]

    def rope(t):
        return t * cos + pltpu.roll(t, LANES - ROT_HALF, 1) * sa + pltpu.roll(t, ROT_HALF, 1) * sb

    for hd in range(ATTN_HEADS):
        sl = slice(hd * HEAD_PAD, (hd + 1) * HEAD_PAD)
        q = rope(proj(Q_OFF + hd * HEAD_PAD, HEAD_PAD))
        q_ref[:, sl] = (q * (HEAD_DIM ** -0.5)).astype(BF16)
        k_ref[:, sl] = rope(proj(K_OFF + hd * HEAD_PAD, HEAD_PAD)).astype(BF16)
    v_ref[...] = proj(V_OFF, ATTN_PAD).astype(BF16)
    gt_ref[...] = 1.0 / (1.0 + jnp.exp(-proj(G_OFF, 2 * D_MODEL)))


def _inproj(x2, shift, scale, g, w_p, cos_t, sa_t, sb_t, *, seq, tm):
    t = x2.shape[0]
    tpb = seq // tm
    row = lambda i: (i, 0)
    per_batch = lambda i: (i // tpb, 0, 0)
    per_pos = lambda i: (i % tpb, 0)
    return pl.pallas_call(
        _inproj_kernel,
        grid=(t // tm,),
        in_specs=[pl.BlockSpec((tm, D_MODEL), row),
                  pl.BlockSpec((1, 1, D_MODEL), per_batch),
                  pl.BlockSpec((1, 1, D_MODEL), per_batch),
                  _const_spec((1, D_MODEL)),
                  _const_spec((D_MODEL, IN_PROJ_PAD)),
                  pl.BlockSpec((tm, LANES), per_pos),
                  pl.BlockSpec((tm, LANES), per_pos),
                  pl.BlockSpec((tm, LANES), per_pos)],
        out_specs=[pl.BlockSpec((tm, POOL_WIDTH), row),
                   pl.BlockSpec((tm, ATTN_PAD), row),
                   pl.BlockSpec((tm, ATTN_PAD), row),
                   pl.BlockSpec((tm, ATTN_PAD), row),
                   pl.BlockSpec((tm, 2 * D_MODEL), row)],
        out_shape=[jax.ShapeDtypeStruct((t, POOL_WIDTH), F32),
                   jax.ShapeDtypeStruct((t, ATTN_PAD), BF16),
                   jax.ShapeDtypeStruct((t, ATTN_PAD), BF16),
                   jax.ShapeDtypeStruct((t, ATTN_PAD), BF16),
                   jax.ShapeDtypeStruct((t, 2 * D_MODEL), F32)],
        compiler_params=_params("arbitrary"),
        name="in_proj",
    )(x2, shift, scale, g, w_p, cos_t, sa_t, sb_t)


def _moba_kernel(q_ref, k_ref, v_ref, bsel_ref, o_ref, qa_scr, ka_scr, va_scr, *, n_blocks,
                 group, heads):
    qi = pl.program_id(2)
    blk = MOBA_BLOCK
    lane = lax.broadcasted_iota(jnp.int32, (blk, LANES), 1)
    lane_f = lane.astype(F32)
    head_lanes = [slice(i * HEAD_PAD, (i + 1) * HEAD_PAD) for i in range(heads)]
    nt = (((1,), (1,)), ((), ()))

    @pl.when(qi == 0)
    def _prepare():
        kms = [jnp.dot(bsel_ref[...], k_ref[:, hl], preferred_element_type=F32)
               for hl in head_lanes]

        def per_block(nb, carry):
            rows = pl.ds(pl.multiple_of(nb * blk, blk), blk)
            valid = (lane >= BIAS_LANE0) & (lane < BIAS_LANE0 + nb)
            bias_lanes = (lane >= BIAS_LANE0) & (lane < BIAS_LANE0 + n_blocks)
            onehot = jnp.where(lane == BIAS_LANE0 + nb, 1.0, 0.0)
            qs = [q_ref[rows, hl].astype(F32) for hl in head_lanes]
            works = [jnp.where(valid, lax.dot_general(q, km, nt, precision=HIGHEST,
                                                      preferred_element_type=F32), -jnp.inf)
                     for q, km in zip(qs, kms)]
            chosen = [jnp.zeros((blk, LANES), F32) for _ in head_lanes]
            for _ in range(MOBA_TOPK):
                ms = [jnp.max(w, axis=1, keepdims=True) for w in works]
                idxs = [jnp.min(jnp.where(w == m, lane_f, 2.0 * LANES), axis=1, keepdims=True)
                        for w, m in zip(works, ms)]
                hits = [(lane_f == idx) & valid for idx in idxs]
                chosen = [jnp.where(hit, 1.0, ch) for hit, ch in zip(hits, chosen)]
                works = [jnp.where(hit, -jnp.inf, w) for hit, w in zip(hits, works)]
            for hl, q, ch in zip(head_lanes, qs, chosen):
                masked = bias_lanes & (ch == 0.0)
                qa_scr[rows, hl] = jnp.where(masked, -MASK_BIG, q).astype(BF16)
                ka_scr[rows, hl] = jnp.where(lane < HEAD_DIM, k_ref[rows, hl].astype(F32),
                                             onehot).astype(BF16)
                va_scr[rows, hl] = jnp.where(lane == ONES_LANE, 1.0,
                                             v_ref[rows, hl].astype(F32)).astype(BF16)
            return carry

        lax.fori_loop(0, n_blocks, per_block, 0)

    own = pl.ds(pl.multiple_of(qi * blk, blk), blk)
    r_i = lax.broadcasted_iota(jnp.int32, (blk, blk), 0)
    c_i = lax.broadcasted_iota(jnp.int32, (blk, blk), 1)
    qas = [qa_scr[own, hl] for hl in head_lanes]
    own_s = [jnp.where(c_i <= r_i,
                       lax.dot_general(jnp.where(lane < HEAD_DIM, qa, jnp.zeros_like(qa)),
                                       ka_scr[own, hl], nt, preferred_element_type=F32),
                       CAUSAL_NEG) for hl, qa in zip(head_lanes, qas)]
    own_m = [jnp.max(s, axis=1, keepdims=True) for s in own_s]
    init = [(m0, jnp.dot(jnp.exp(s - m0).astype(BF16), va_scr[own, hl],
                         preferred_element_type=F32))
            for hl, s, m0 in zip(head_lanes, own_s, own_m)]

    span = group * blk

    def past(j, carry):
        rows = pl.ds(pl.multiple_of(j * span, span), span)
        scores = [lax.dot_general(qa, ka_scr[rows, hl], nt, preferred_element_type=F32)
                  for hl, qa in zip(head_lanes, qas)]
        m_new = [jnp.maximum(m, jnp.max(sj, axis=1, keepdims=True))
                 for sj, (m, _) in zip(scores, carry)]
        out = []
        for hl, sj, mn, (m, acc) in zip(head_lanes, scores, m_new, carry):
            pj = jnp.exp(sj - mn)
            acc = acc * jnp.exp(m - mn) + jnp.dot(pj.astype(BF16), va_scr[rows, hl],
                                                  preferred_element_type=F32)
            out.append((mn, acc))
        return tuple(out)

    final = lax.fori_loop(0, (qi + group - 1) // group, past, tuple(init))
    for hl, (_, acc) in zip(head_lanes, final):
        o_ref[:, hl] = (acc / acc[:, ONES_LANE:ONES_LANE + 1]).astype(BF16)


def _moba(qp, kp, vp, bsel, *, batch, seq, group, heads):
    n_blocks = seq // MOBA_BLOCK
    assert n_blocks % group == 0 and ATTN_HEADS % heads == 0
    width = heads * HEAD_PAD
    head = lambda b, h, i: (b, h)
    return pl.pallas_call(
        functools.partial(_moba_kernel, n_blocks=n_blocks, group=group, heads=heads),
        grid=(batch, ATTN_HEADS // heads, n_blocks),
        in_specs=[pl.BlockSpec((seq, width), head),
                  pl.BlockSpec((seq, width), head),
                  pl.BlockSpec((seq, width), head),
                  _const_spec((LANES, seq))],
        out_specs=pl.BlockSpec((MOBA_BLOCK, width), lambda b, h, i: (b * n_blocks + i, h)),
        out_shape=jax.ShapeDtypeStruct((batch * seq, ATTN_PAD), BF16),
        scratch_shapes=[pltpu.VMEM((seq, width), BF16)] * 3,
        compiler_params=_params("arbitrary", "arbitrary", "arbitrary"),
        name="moba_attn",
    )(qp, kp, vp, bsel)


def _merge_kernel(u_ref, uprev_ref, attn_ref, gt_ref, x_ref, poolw_ref, pscale_ref, wp_ref,
                  wa_ref, wo_ref, gate1_ref, g2_ref, sh2_ref, sc2_ref, x1_ref, h2t_ref,
                  *, tm, tpb):
    i = pl.program_id(0)
    first = (i % tpb) == 0
    pos = lax.broadcasted_iota(jnp.int32, (tm, LANES), 0) + (i % tpb) * tm
    mixed = []
    for g, w in enumerate(POOL_WINDOWS):
        sl = slice(g * POOL_GROUP_DIM, (g + 1) * POOL_GROUP_DIM)
        u = u_ref[:, sl]
        halo = jnp.where(first, 0.0, uprev_ref[:, sl])
        ext = jnp.concatenate([halo, u], axis=0)
        span = 1
        while span < w:
            ext = ext + pltpu.roll(ext, span, 0)
            span *= 2
        cnt = jnp.minimum(pos + 1, w).astype(F32)
        diff = ext[POOL_HALO:, :] / cnt - u
        m = jnp.dot(diff.astype(BF16), poolw_ref[g], preferred_element_type=F32)
        mixed.append((m * pscale_ref[:, sl]).astype(BF16))
    pooled = jnp.concatenate(mixed, axis=1)
    bp = jnp.dot(pooled, wp_ref[...], preferred_element_type=F32)
    ba = jnp.dot(attn_ref[...], wa_ref[...], preferred_element_type=F32)
    merged = gt_ref[:, :D_MODEL] * bp + gt_ref[:, D_MODEL:] * ba
    y = jnp.dot(merged.astype(BF16), wo_ref[...], preferred_element_type=F32)
    x1 = x_ref[...] + gate1_ref[0] * y
    x1_ref[...] = x1
    h2 = _rms_modulate(x1, g2_ref[...], sh2_ref[0], sc2_ref[0])
    h2t_ref[...] = h2.T.astype(BF16)


def _merge(u_pool, attn, gates, x2, poolw, pscale, wp, wa, wo, gate1, g2, shift2, scale2,
           *, seq, tm):
    t = x2.shape[0]
    tpb = seq // tm
    row = lambda i: (i, 0)
    per_batch = lambda i: (i // tpb, 0, 0)
    halo_blocks = tm // POOL_HALO
    return pl.pallas_call(
        functools.partial(_merge_kernel, tm=tm, tpb=tpb),
        grid=(t // tm,),
        in_specs=[pl.BlockSpec((tm, POOL_WIDTH), row),
                  pl.BlockSpec((POOL_HALO, POOL_WIDTH),
                               lambda i: (jnp.maximum(i * halo_blocks - 1, 0), 0)),
                  pl.BlockSpec((tm, ATTN_PAD), row),
                  pl.BlockSpec((tm, 2 * D_MODEL), row),
                  pl.BlockSpec((tm, D_MODEL), row),
                  _const_spec((len(POOL_WINDOWS), POOL_GROUP_DIM, POOL_GROUP_DIM)),
                  _const_spec((1, POOL_WIDTH)),
                  _const_spec((POOL_WIDTH, D_MODEL)),
                  _const_spec((ATTN_PAD, D_MODEL)),
                  _const_spec((D_MODEL, D_MODEL)),
                  pl.BlockSpec((1, 1, D_MODEL), per_batch),
                  _const_spec((1, D_MODEL)),
                  pl.BlockSpec((1, 1, D_MODEL), per_batch),
                  pl.BlockSpec((1, 1, D_MODEL), per_batch)],
        out_specs=[pl.BlockSpec((tm, D_MODEL), row),
                   pl.BlockSpec((D_MODEL, tm), lambda i: (0, i))],
        out_shape=[jax.ShapeDtypeStruct((t, D_MODEL), F32),
                   jax.ShapeDtypeStruct((D_MODEL, t), BF16)],
        compiler_params=_params("arbitrary"),
        name="mixer_merge",
    )(u_pool, u_pool, attn, gates, x2, poolw, pscale, wp, wa, wo, gate1, g2, shift2, scale2)


def _extract_top16(streams):
    width = streams[0][0].shape[1]
    slot = lax.broadcasted_iota(jnp.int32, (PEER_TOPK, width), 0)

    def body(r, carry):
        out = []
        for (work, rank, vals), (_, index_f) in zip(carry, streams):
            m = jnp.max(work, axis=0, keepdims=True)
            idx = jnp.min(jnp.where(work == m, index_f, jnp.inf), axis=0, keepdims=True)
            hit = index_f == idx
            out.append((jnp.where(hit, -jnp.inf, work),
                        jnp.where(hit, lax.convert_element_type(r, F32), rank),
                        jnp.where(slot == r, m, vals)))
        return tuple(out)

    init = tuple((sc, jnp.full(sc.shape, NOT_MEMBER, F32), jnp.zeros((PEER_TOPK, width), F32))
                 for sc, _ in streams)
    return [(vals, rank) for _, rank, vals in lax.fori_loop(0, PEER_TOPK, body, init)]


def _batcher_pairs(n):
    pairs, p = [], 1
    while p < n:
        k = p
        while k >= 1:
            for j in range(k % p, n - k, 2 * k):
                for i in range(min(k, n - j - k)):
                    if (i + j) // (2 * p) == (i + j + k) // (2 * p):
                        pairs.append((i + j, i + j + k))
            k //= 2
        p *= 2
    return pairs


def _sorted_top16(regs):
    x = list(regs)

    def exchange(i, j):
        hi, lo = x[i], x[j]
        if lo is None:
            return
        if hi is None:
            x[i], x[j] = lo, None
        else:
            x[i], x[j] = jnp.maximum(hi, lo), jnp.minimum(hi, lo)

    for i, j in _batcher_pairs(PEER_TOPK):
        exchange(i, j)
    for shift in (4, 2, 1):
        other = [None if v is None else pltpu.roll(v, shift, 0) for v in x]
        for i in range(PEER_TOPK):
            mine, theirs = x[i], other[PEER_TOPK - 1 - i]
            x[i] = theirs if mine is None else mine if theirs is None else jnp.maximum(mine, theirs)
        dist = PEER_TOPK // 2
        while dist >= 1:
            for i in range(PEER_TOPK):
                if i & dist == 0:
                    exchange(i, i + dist)
            dist //= 2
    return x


def _route_kernel(h2t_ref, wqt_ref, keys_ref, r2_ref, e1_ref, n_ref, c_ref, qt_scr, *, tm):
    qt_scr[...] = jnp.dot(wqt_ref[...], h2t_ref[...], preferred_element_type=F32)
    half = PEER_TOPK // 2
    width = LANES

    def iota_f(rows):
        return lax.broadcasted_iota(jnp.int32, (rows, width), 0).astype(F32)

    tail = PEER_TOPK + (half - 1) * half
    slot = lax.broadcasted_iota(jnp.int32, (half, width), 0)

    def candidates(a, b):
        return jnp.concatenate([a[0:1, :] + b]
                               + [a[r:r + 1, :] + b[:half, :] for r in range(1, half)]
                               + [a[half:, :] + b[0:1, :]], axis=0)

    def staircase(chosen, a, b):
        e0 = jnp.exp(a - a[0:1, :])
        e1 = jnp.exp(b - b[0:1, :])
        n_lo = jnp.zeros((half, width), F32)
        pref_lo = jnp.zeros((half, width), F32)
        for r in range(half):
            lo = 0 if r == 0 else PEER_TOPK + (r - 1) * half
            cnt = PEER_TOPK if r == 0 else half
            grp = chosen[lo:lo + cnt, :]
            n_lo = jnp.where(slot == r, jnp.sum(grp, axis=0, keepdims=True), n_lo)
            pref_lo = jnp.where(slot == r, jnp.sum(grp * e1[:cnt, :], axis=0, keepdims=True),
                                pref_lo)
        n = jnp.concatenate([n_lo, chosen[tail:, :]], axis=0)
        pref = jnp.concatenate([pref_lo, chosen[tail:, :]], axis=0)
        return n, jnp.sum(e0 * pref, axis=0, keepdims=True)

    def emit(h, cols, s0, s1, a, b, rank2, n_dense, z):
        r2_ref[h, :, cols] = rank2
        e1_ref[h, :, cols] = jnp.exp(s1 - b[0:1, :])
        n_ref[h, :, cols] = n_dense
        c_ref[h, :, cols] = 0.5 * jnp.exp(s0 - a[0:1, :]) / z

    def per_head(h, carry):
        for cc in range(tm // width):
            route(h, slice(cc * width, (cc + 1) * width))
        return carry

    def route(h, cols):
        def half_scores(p):
            rows = pl.ds(pl.multiple_of((2 * h + p) * PEER_HALF, PEER_HALF), PEER_HALF)
            return jnp.dot(keys_ref[2 * h + p], qt_scr[rows, cols], precision=HIGHEST,
                           preferred_element_type=F32)

        s0 = half_scores(0)
        s1 = half_scores(1)

        pieces = PEER_NKEYS // SUBLANES
        s0v = [s0[v * SUBLANES:(v + 1) * SUBLANES, :] for v in range(pieces)]
        s1v = [s1[v * SUBLANES:(v + 1) * SUBLANES, :] for v in range(pieces)]
        a = _sorted_top16(s0v)
        b = _sorted_top16(s1v)
        sub = lax.broadcasted_iota(jnp.int32, (SUBLANES, width), 0)

        def stack(vals):
            out = vals[0]
            for j in range(1, SUBLANES):
                out = jnp.where(sub == j, vals[j], out)
            return out

        def total(x):
            return jnp.broadcast_to(jnp.sum(x, axis=0, keepdims=True), (SUBLANES, width))

        b_lo, b_hi, a_hi = stack(b[:half]), stack(b[half:]), stack(a[half:])
        cand = ([a[0] + b_lo, a[0] + b_hi] + [a[r] + b_lo for r in range(1, half)]
                + [a_hi + b[0]])
        top = _sorted_top16(cand + [None] * (PEER_TOPK - len(cand)))
        chosen = [jnp.where(c >= top[-1], 1.0, 0.0) for c in cand]
        n = ([total(chosen[0] + chosen[1])] + [total(chosen[1 + r]) for r in range(1, half)]
             + [total(jnp.where(sub == j, chosen[-1], 0.0)) for j in range(half)])
        z = total(sum(ch * jnp.exp(c - top[0]) for ch, c in zip(chosen, cand)))
        rank2, n_dense = [], []
        for v in range(pieces):
            r2, nd = jnp.full((SUBLANES, width), NOT_MEMBER, F32), jnp.zeros((SUBLANES, width), F32)
            for r in range(PEER_TOPK):
                r2 = jnp.where(s1v[v] == b[r], float(r), r2)
                nd = jnp.where(s0v[v] == a[r], n[r], nd)
            rank2.append(r2)
            n_dense.append(nd)
        emit(h, cols, s0, s1, a[0], b[0], jnp.concatenate(rank2, axis=0),
             jnp.concatenate(n_dense, axis=0), z[0:1, :])

        distinct = jnp.ones((SUBLANES, width), F32)
        for lst in (a, b, top):
            for r in range(PEER_TOPK - 1):
                distinct = jnp.where(lst[r] == lst[r + 1], 0.0, distinct)
        at_least = lambda regs, thr: total(sum(jnp.where(x >= thr, 1.0, 0.0) for x in regs))
        sixteen = ((at_least(s0v, a[-1]) == PEER_TOPK) & (at_least(s1v, b[-1]) == PEER_TOPK)
                   & (total(sum(chosen)) == PEER_TOPK))
        tie_free = jnp.min(jnp.where(sixteen, distinct, 0.0))

        @pl.when(tie_free < 0.5)
        def _exact():
            key_f = iota_f(PEER_NKEYS)
            pair_f = jnp.concatenate(
                [iota_f(PEER_TOPK)]
                + [iota_f(half) + float(r * PEER_TOPK) for r in range(1, half)]
                + [(iota_f(half) + float(half)) * float(PEER_TOPK)], axis=0)
            (a, rank1), (b, rank2) = _extract_top16([(s0, key_f), (s1, key_f)])
            ((_, pick),) = _extract_top16([(candidates(a, b), pair_f)])
            n, z = staircase(jnp.where(pick < NOT_MEMBER, 1.0, 0.0), a, b)
            n_dense = jnp.zeros((PEER_NKEYS, width), F32)
            for r in range(PEER_TOPK):
                n_dense = jnp.where(rank1 == float(r), n[r:r + 1, :], n_dense)
            emit(h, cols, s0, s1, a, b, rank2, n_dense, z)

    lax.fori_loop(0, PEER_HEADS, per_head, 0)


def _route(h2t, wqt, keys, *, tm):
    t = h2t.shape[1]
    dense = jax.ShapeDtypeStruct((PEER_HEADS, PEER_NKEYS, t), F32)
    dense_spec = pl.BlockSpec((PEER_HEADS, PEER_NKEYS, tm), lambda i: (0, 0, i))
    return pl.pallas_call(
        functools.partial(_route_kernel, tm=tm),
        grid=(t // tm,),
        in_specs=[pl.BlockSpec((D_MODEL, tm), lambda i: (0, i)),
                  _const_spec((2 * PEER_HEADS * PEER_HALF, D_MODEL)),
                  _const_spec((2 * PEER_HEADS, PEER_NKEYS, PEER_HALF))],
        out_specs=[dense_spec] * 4,
        out_shape=[dense] * 4,
        scratch_shapes=[pltpu.VMEM((2 * PEER_HEADS * PEER_HALF, tm), F32)],
        compiler_params=_params("arbitrary"),
        name="peer_route",
    )(h2t, wqt, keys)


def _experts_kernel(h2t_ref, u_ref, vt_ref, r2_ref, e1_ref, n_ref, c_ref, x1_ref, gate2_ref,
                    gf_ref, o_ref, acc_scr, act_a, act_b, p_a, p_b, *, tm, sub_blocks):
    s = pl.program_id(1)
    last = pl.num_programs(1) - 1

    @pl.when(s == 0)
    def _fill():
        acc_scr[...] = jnp.zeros_like(acc_scr)
        p_b[...] = jnp.zeros_like(p_b)
        act_a[...] = jnp.dot(u_ref[...], h2t_ref[...], preferred_element_type=F32)

    quad = 4 * SUBLANES

    def step(act_new, act_old, p_new, p_old):
        n_cc = tm // LANES
        n_kq = PEER_NKEYS // quad
        eb = act_new.shape[0]
        k_split = D_MODEL // MXU_DEPTH
        m_split = MATMUL_ROW_CHUNKS
        m1, k1 = eb // m_split, D_MODEL // k_split
        m2, k2 = D_MODEL // m_split, eb // k_split
        stride = 2 * n_cc * n_kq // (m_split * k_split)

        def matmul_pieces(slot):
            if slot % stride:
                return
            piece = slot // stride
            mi, ki = divmod(piece, k_split)
            u_rows = slice(mi * m1, (mi + 1) * m1)
            ks = slice(ki * k1, (ki + 1) * k1)
            part = jnp.dot(u_ref[u_rows, ks], h2t_ref[ks, :], preferred_element_type=F32)
            if ki == 0:
                act_new[u_rows, :] = part
            else:
                act_new[u_rows, :] += part
            d_rows = slice(mi * m2, (mi + 1) * m2)
            ks = slice(ki * k2, (ki + 1) * k2)
            acc_scr[d_rows, :] += jnp.dot(vt_ref[d_rows, ks], p_old[ks, :],
                                          preferred_element_type=F32)

        for cc in range(n_cc):
            cols = slice(cc * LANES, (cc + 1) * LANES)
            for kq in range(n_kq):
                block = cc * n_kq + kq
                matmul_pieces(2 * block)
                i2 = slice(kq * quad, (kq + 1) * quad)
                w = [jnp.zeros((quad, LANES), F32) for _ in range(sub_blocks)]
                for h in range(PEER_HEADS):
                    r2 = r2_ref[h, i2, cols]
                    e1 = e1_ref[h, i2, cols]
                    for jb in range(sub_blocks):
                        keep = r2 < n_ref[h, jb:jb + 1, cols]
                        w[jb] = w[jb] + jnp.where(keep, e1 * c_ref[h, jb:jb + 1, cols], 0.0)
                matmul_pieces(2 * block + 1)
                for jb in range(sub_blocks):
                    rows = slice(jb * PEER_NKEYS + kq * quad, jb * PEER_NKEYS + (kq + 1) * quad)
                    a = act_old[rows, cols]
                    p_new[rows, cols] = (w[jb] * (a * (1.0 + lax.erf(a * SQRT_HALF)))).astype(BF16)

    @pl.when((s % 2 == 0) & (s > 0))
    def _even():
        step(act_a, act_b, p_b, p_a)

    @pl.when((s % 2 == 1) & (s < last))
    def _odd():
        step(act_b, act_a, p_a, p_b)

    @pl.when(s == last)
    def _drain():
        acc_scr[...] += jnp.dot(vt_ref[...], p_b[...], preferred_element_type=F32)
        x2 = x1_ref[...] + gate2_ref[0] * acc_scr[...].T
        ms = jnp.mean(x2 * x2, axis=-1, keepdims=True)
        o_ref[...] = x2 * lax.rsqrt(ms + EPS) * gf_ref[...]


def _experts(h2t, u, v, r2, e1, n, c, x1, gate2, gf, *, seq, tm, sub_blocks):
    t = h2t.shape[1]
    tpb = seq // tm
    eb = sub_blocks * PEER_NKEYS
    n_eb = PEER_EXPERTS // eb
    assert n_eb % 2 == 0
    u_b = u.astype(BF16)
    vt_b = v.reshape(n_eb, eb, D_MODEL).transpose(0, 2, 1).astype(BF16)
    tok3 = lambda i, s: (0, 0, i)
    blk = lambda s, lag: jnp.clip(s - lag, 0, n_eb - 1)
    return pl.pallas_call(
        functools.partial(_experts_kernel, tm=tm, sub_blocks=sub_blocks),
        grid=(t // tm, n_eb + 2),
        in_specs=[pl.BlockSpec((D_MODEL, tm), lambda i, s: (0, i)),
                  pl.BlockSpec((eb, D_MODEL), lambda i, s: (blk(s, 0), 0)),
                  pl.BlockSpec((None, D_MODEL, eb), lambda i, s: (blk(s, 2), 0, 0)),
                  pl.BlockSpec((PEER_HEADS, PEER_NKEYS, tm), tok3),
                  pl.BlockSpec((PEER_HEADS, PEER_NKEYS, tm), tok3),
                  pl.BlockSpec((PEER_HEADS, sub_blocks, tm), lambda i, s: (0, blk(s, 1), i)),
                  pl.BlockSpec((PEER_HEADS, sub_blocks, tm), lambda i, s: (0, blk(s, 1), i)),
                  pl.BlockSpec((tm, D_MODEL), lambda i, s: (i, 0)),
                  pl.BlockSpec((1, 1, D_MODEL), lambda i, s: (i // tpb, 0, 0)),
                  pl.BlockSpec((1, D_MODEL), lambda i, s: (0, 0))],
        out_specs=pl.BlockSpec((tm, D_MODEL), lambda i, s: (i, 0)),
        out_shape=jax.ShapeDtypeStruct((t, D_MODEL), F32),
        scratch_shapes=[pltpu.VMEM((D_MODEL, tm), F32),
                        pltpu.VMEM((eb, tm), F32), pltpu.VMEM((eb, tm), F32),
                        pltpu.VMEM((eb, tm), BF16), pltpu.VMEM((eb, tm), BF16)],
        compiler_params=_params("arbitrary", "arbitrary"),
        name="peer_experts",
    )(h2t, u_b, vt_b, r2, e1, n, c, x1, gate2, gf)


def _pad_heads(w):
    d = w.shape[0]
    w = w.reshape(d, ATTN_HEADS, HEAD_DIM)
    return jnp.pad(w, ((0, 0), (0, 0), (0, HEAD_PAD - HEAD_DIM))).reshape(d, ATTN_PAD)


def _rope_tables(seq):
    inv = ROPE_THETA ** (-jnp.arange(ROT_HALF, dtype=F32) / ROT_HALF)
    ang = jnp.arange(seq).astype(F32)[:, None] * inv[None, :]
    cos, sin = jnp.cos(ang), jnp.sin(ang)
    z = lambda n: jnp.zeros((seq, n), F32)
    cos_t = jnp.concatenate([cos, cos, jnp.ones((seq, LANES - 2 * ROT_HALF), F32)], axis=1)
    sa_t = jnp.concatenate([-sin, z(LANES - ROT_HALF)], axis=1)
    sb_t = jnp.concatenate([z(ROT_HALF), sin, z(LANES - 2 * ROT_HALF)], axis=1)
    return cos_t, sa_t, sb_t


def _block_mean_rows(seq):
    r = jnp.arange(LANES)[:, None]
    s = jnp.arange(seq)[None, :]
    return jnp.where(r - BIAS_LANE0 == s // MOBA_BLOCK, 1.0 / MOBA_BLOCK, 0.0).astype(BF16)


def kernel(x, c, w_ada, b_ada, norm_mix_g, w_in, pool_w, pool_scale, w_branch_pool,
           w_branch_attn, w_out, norm_ffn_g, peer_wq, peer_sub_keys, peer_u, peer_v,
           norm_final_g):
    batch, seq, d = x.shape
    depth = w_ada.shape[0]
    assert d == D_MODEL and batch <= 8 and seq % 512 == 0
    assert seq // MOBA_BLOCK <= LANES - BIAS_LANE0
    assert depth == 1
    t = batch * seq
    x2 = x.reshape(t, d)
    c8 = jnp.pad(c, ((0, 8 - batch), (0, 0)))
    cos_t, sa_t, sb_t = _rope_tables(seq)
    bsel = _block_mean_rows(seq)
    row1 = lambda v: v.reshape(1, -1)
    per_batch = lambda v: v.reshape(batch, 1, d)

    for l in range(depth):
        mod = _ada(c8, w_ada[l], row1(b_ada[l]))[:batch]
        shift1, scale1, gate1, shift2, scale2, gate2 = [per_batch(m) for m in
                                                        jnp.split(mod, 6, axis=-1)]
        wl = w_in[l]
        w_p = jnp.concatenate(
            [wl[:, :POOL_WIDTH],
             _pad_heads(wl[:, POOL_WIDTH:POOL_WIDTH + ATTN_WIDTH]),
             _pad_heads(wl[:, POOL_WIDTH + ATTN_WIDTH:POOL_WIDTH + 2 * ATTN_WIDTH]),
             _pad_heads(wl[:, POOL_WIDTH + 2 * ATTN_WIDTH:POOL_WIDTH + 3 * ATTN_WIDTH]),
             wl[:, POOL_WIDTH + 3 * ATTN_WIDTH:]], axis=1).astype(BF16)
        u_pool, qp, kp, vp, gates = _inproj(x2, shift1, scale1, row1(norm_mix_g[l]), w_p,
                                            cos_t, sa_t, sb_t, seq=seq, tm=256)
        attn = _moba(qp, kp, vp, bsel, batch=batch, seq=seq, group=8, heads=2)
        wa_p = _pad_heads(w_branch_attn[l].T).T.astype(BF16)
        x1, h2t = _merge(u_pool, attn, gates, x2, pool_w[l].astype(BF16), row1(pool_scale[l]),
                         w_branch_pool[l].astype(BF16), wa_p, w_out[l].astype(BF16), gate1,
                         row1(norm_ffn_g[l]), shift2, scale2, seq=seq, tm=256)
        keys = peer_sub_keys[l].reshape(2 * PEER_HEADS, PEER_NKEYS, PEER_HALF)
        r2, e1, n, cden = _route(h2t, peer_wq[l].T.astype(BF16), keys, tm=256)
        x2 = _experts(h2t, peer_u[l], peer_v[l], r2, e1, n, cden,
                      x1, gate2, row1(norm_final_g), seq=seq, tm=512, sub_blocks=8)
    return x2.reshape(batch, seq, d)
```

```python
import functools
import math

import jax
import jax.numpy as jnp
import numpy as np
from jax import lax
from jax.experimental import pallas as pl
from jax.experimental.pallas import tpu as pltpu

F32 = jnp.float32
BF16 = jnp.bfloat16
HIGHEST = lax.Precision.HIGHEST

D_MODEL = 1024
POOL_WINDOWS = (2, 4, 8, 16)
POOL_WIDTH = 512
POOL_GROUP_DIM = 128
POOL_HALO = 16
ATTN_HEADS = 8
HEAD_DIM = 64
ATTN_WIDTH = 512
MOBA_BLOCK = 256
MOBA_TOPK = 3
ROPE_THETA = 500000.0
ROT_HALF = 8
PEER_HEADS = 8
PEER_NKEYS = 128
PEER_EXPERTS = PEER_NKEYS * PEER_NKEYS
PEER_HALF = 128
PEER_TOPK = 16
EPS = 1e-6

LANES = 128
SUBLANES = 8
MXU_DEPTH = 256
MATMUL_ROW_CHUNKS = 4
HEAD_PAD = LANES
ATTN_PAD = ATTN_HEADS * HEAD_PAD
BIAS_LANE0 = HEAD_DIM
ONES_LANE = HEAD_DIM
MASK_BIG = 2.0 ** 100
CAUSAL_NEG = -1e30
NOT_MEMBER = 255.0
SQRT_HALF = float(np.sqrt(0.5).astype(np.float32))
VMEM_LIMIT = 56 * 1024 * 1024

IN_PROJ_PAD = POOL_WIDTH + 3 * ATTN_PAD + 2 * D_MODEL
Q_OFF = POOL_WIDTH
K_OFF = Q_OFF + ATTN_PAD
V_OFF = K_OFF + ATTN_PAD
G_OFF = V_OFF + ATTN_PAD


def _params(*semantics):
    return pltpu.CompilerParams(dimension_semantics=semantics, vmem_limit_bytes=VMEM_LIMIT)


def _const_spec(shape):
    nd = len(shape)
    return pl.BlockSpec(shape, lambda *_: (0,) * nd, pipeline_mode=pl.Buffered(1))


def _rms_modulate(x, g, shift, scale):
    ms = jnp.mean(x * x, axis=-1, keepdims=True)
    return (x * lax.rsqrt(ms + EPS) * g) * (1.0 + scale) + shift


def _ada_kernel(c_ref, w_ref, b_ref, o_ref):
    c = c_ref[...]
    act = c / (1.0 + jnp.exp(-c))
    o_ref[...] = jnp.dot(act, w_ref[...], precision=HIGHEST,
                         preferred_element_type=F32) + b_ref[...]


def _ada(c8, w, b):
    n = w.shape[1]
    tn = 1536
    return pl.pallas_call(
        _ada_kernel,
        grid=(n // tn,),
        in_specs=[pl.BlockSpec((8, D_MODEL), lambda j: (0, 0)),
                  pl.BlockSpec((D_MODEL, tn), lambda j: (0, j)),
                  pl.BlockSpec((1, tn), lambda j: (0, j))],
        out_specs=pl.BlockSpec((8, tn), lambda j: (0, j)),
        out_shape=jax.ShapeDtypeStruct((8, n), F32),
        compiler_params=_params("arbitrary"),
        name="ada_mod",
    )(c8, w, b)


def _inproj_kernel(x_ref, sh_ref, sc_ref, g_ref, w_ref, cos_ref, sa_ref, sb_ref,
                   u_ref, q_ref, k_ref, v_ref, gt_ref):
    h = _rms_modulate(x_ref[...], g_ref[...], sh_ref[0], sc_ref[0]).astype(BF16)

    def proj(off, width):
        return jnp.dot(h, w_ref[:, off:off + width], preferred_element_type=F32)

    u_ref[...] = proj(0, POOL_WIDTH)
    cos, sa, sb = cos_ref[...], sa_ref[...], sb_ref[...]

    def rope(t):
        return t * cos + pltpu.roll(t, LANES - ROT_HALF, 1) * sa + pltpu.roll(t, ROT_HALF, 1) * sb

    for hd in range(ATTN_HEADS):
        sl = slice(hd * HEAD_PAD, (hd + 1) * HEAD_PAD)
        q = rope(proj(Q_OFF + hd * HEAD_PAD, HEAD_PAD))
        q_ref[:, sl] = (q * (HEAD_DIM ** -0.5)).astype(BF16)
        k_ref[:, sl] = rope(proj(K_OFF + hd * HEAD_PAD, HEAD_PAD)).astype(BF16)
    v_ref[...] = proj(V_OFF, ATTN_PAD).astype(BF16)
    gt_ref[...] = 1.0 / (1.0 + jnp.exp(-proj(G_OFF, 2 * D_MODEL)))


def _inproj(x2, shift, scale, g, w_p, cos_t, sa_t, sb_t, *, seq, tm):
    t = x2.shape[0]
    tpb = seq // tm
    row = lambda i: (i, 0)
    per_batch = lambda i: (i // tpb, 0, 0)
    per_pos = lambda i: (i % tpb, 0)
    return pl.pallas_call(
        _inproj_kernel,
        grid=(t // tm,),
        in_specs=[pl.BlockSpec((tm, D_MODEL), row),
                  pl.BlockSpec((1, 1, D_MODEL), per_batch),
                  pl.BlockSpec((1, 1, D_MODEL), per_batch),
                  _const_spec((1, D_MODEL)),
                  _const_spec((D_MODEL, IN_PROJ_PAD)),
                  pl.BlockSpec((tm, LANES), per_pos),
                  pl.BlockSpec((tm, LANES), per_pos),
                  pl.BlockSpec((tm, LANES), per_pos)],
        out_specs=[pl.BlockSpec((tm, POOL_WIDTH), row),
                   pl.BlockSpec((tm, ATTN_PAD), row),
                   pl.BlockSpec((tm, ATTN_PAD), row),
                   pl.BlockSpec((tm, ATTN_PAD), row),
                   pl.BlockSpec((tm, 2 * D_MODEL), row)],
        out_shape=[jax.ShapeDtypeStruct((t, POOL_WIDTH), F32),
                   jax.ShapeDtypeStruct((t, ATTN_PAD), BF16),
                   jax.ShapeDtypeStruct((t, ATTN_PAD), BF16),
                   jax.ShapeDtypeStruct((t, ATTN_PAD), BF16),
                   jax.ShapeDtypeStruct((t, 2 * D_MODEL), F32)],
        compiler_params=_params("arbitrary"),
        name="in_proj",
    )(x2, shift, scale, g, w_p, cos_t, sa_t, sb_t)


def _moba_kernel(q_ref, k_ref, v_ref, bsel_ref, o_ref, qa_scr, ka_scr, va_scr, *, n_blocks,
                 group, heads):
    qi = pl.program_id(2)
    blk = MOBA_BLOCK
    lane = lax.broadcasted_iota(jnp.int32, (blk, LANES), 1)
    lane_f = lane.astype(F32)
    head_lanes = [slice(i * HEAD_PAD, (i + 1) * HEAD_PAD) for i in range(heads)]
    nt = (((1,), (1,)), ((), ()))

    @pl.when(qi == 0)
    def _prepare():
        kms = [jnp.dot(bsel_ref[...], k_ref[:, hl], preferred_element_type=F32)
               for hl in head_lanes]

        def per_block(nb, carry):
            rows = pl.ds(pl.multiple_of(nb * blk, blk), blk)
            valid = (lane >= BIAS_LANE0) & (lane < BIAS_LANE0 + nb)
            bias_lanes = (lane >= BIAS_LANE0) & (lane < BIAS_LANE0 + n_blocks)
            onehot = jnp.where(lane == BIAS_LANE0 + nb, 1.0, 0.0)
            qs = [q_ref[rows, hl].astype(F32) for hl in head_lanes]
            works = [jnp.where(valid, lax.dot_general(q, km, nt, precision=HIGHEST,
                                                      preferred_element_type=F32), -jnp.inf)
                     for q, km in zip(qs, kms)]
            chosen = [jnp.zeros((blk, LANES), F32) for _ in head_lanes]
            for _ in range(MOBA_TOPK):
                ms = [jnp.max(w, axis=1, keepdims=True) for w in works]
                idxs = [jnp.min(jnp.where(w == m, lane_f, 2.0 * LANES), axis=1, keepdims=True)
                        for w, m in zip(works, ms)]
                hits = [(lane_f == idx) & valid for idx in idxs]
                chosen = [jnp.where(hit, 1.0, ch) for hit, ch in zip(hits, chosen)]
                works = [jnp.where(hit, -jnp.inf, w) for hit, w in zip(hits, works)]
            for hl, q, ch in zip(head_lanes, qs, chosen):
                masked = bias_lanes & (ch == 0.0)
                qa_scr[rows, hl] = jnp.where(masked, -MASK_BIG, q).astype(BF16)
                ka_scr[rows, hl] = jnp.where(lane < HEAD_DIM, k_ref[rows, hl].astype(F32),
                                             onehot).astype(BF16)
                va_scr[rows, hl] = jnp.where(lane == ONES_LANE, 1.0,
                                             v_ref[rows, hl].astype(F32)).astype(BF16)
            return carry

        lax.fori_loop(0, n_blocks, per_block, 0)

    own = pl.ds(pl.multiple_of(qi * blk, blk), blk)
    r_i = lax.broadcasted_iota(jnp.int32, (blk, blk), 0)
    c_i = lax.broadcasted_iota(jnp.int32, (blk, blk), 1)
    qas = [qa_scr[own, hl] for hl in head_lanes]
    own_s = [jnp.where(c_i <= r_i,
                       lax.dot_general(jnp.where(lane < HEAD_DIM, qa, jnp.zeros_like(qa)),
                                       ka_scr[own, hl], nt, preferred_element_type=F32),
                       CAUSAL_NEG) for hl, qa in zip(head_lanes, qas)]
    own_m = [jnp.max(s, axis=1, keepdims=True) for s in own_s]
    init = [(m0, jnp.dot(jnp.exp(s - m0).astype(BF16), va_scr[own, hl],
                         preferred_element_type=F32))
            for hl, s, m0 in zip(head_lanes, own_s, own_m)]

    span = group * blk

    def past(j, carry):
        rows = pl.ds(pl.multiple_of(j * span, span), span)
        scores = [lax.dot_general(qa, ka_scr[rows, hl], nt, preferred_element_type=F32)
                  for hl, qa in zip(head_lanes, qas)]
        m_new = [jnp.maximum(m, jnp.max(sj, axis=1, keepdims=True))
                 for sj, (m, _) in zip(scores, carry)]
        out = []
        for hl, sj, mn, (m, acc) in zip(head_lanes, scores, m_new, carry):
            pj = jnp.exp(sj - mn)
            acc = acc * jnp.exp(m - mn) + jnp.dot(pj.astype(BF16), va_scr[rows, hl],
                                                  preferred_element_type=F32)
            out.append((mn, acc))
        return tuple(out)

    final = lax.fori_loop(0, (qi + group - 1) // group, past, tuple(init))
    for hl, (_, acc) in zip(head_lanes, final):
        o_ref[:, hl] = (acc / acc[:, ONES_LANE:ONES_LANE + 1]).astype(BF16)


def _moba(qp, kp, vp, bsel, *, batch, seq, group, heads):
    n_blocks = seq // MOBA_BLOCK
    assert n_blocks % group == 0 and ATTN_HEADS % heads == 0
    width = heads * HEAD_PAD
    head = lambda b, h, i: (b, h)
    return pl.pallas_call(
        functools.partial(_moba_kernel, n_blocks=n_blocks, group=group, heads=heads),
        grid=(batch, ATTN_HEADS // heads, n_blocks),
        in_specs=[pl.BlockSpec((seq, width), head),
                  pl.BlockSpec((seq, width), head),
                  pl.BlockSpec((seq, width), head),
                  _const_spec((LANES, seq))],
        out_specs=pl.BlockSpec((MOBA_BLOCK, width), lambda b, h, i: (b * n_blocks + i, h)),
        out_shape=jax.ShapeDtypeStruct((batch * seq, ATTN_PAD), BF16),
        scratch_shapes=[pltpu.VMEM((seq, width), BF16)] * 3,
        compiler_params=_params("arbitrary", "arbitrary", "arbitrary"),
        name="moba_attn",
    )(qp, kp, vp, bsel)


def _merge_kernel(u_ref, uprev_ref, attn_ref, gt_ref, x_ref, poolw_ref, pscale_ref, wp_ref,
                  wa_ref, wo_ref, gate1_ref, g2_ref, sh2_ref, sc2_ref, x1_ref, h2t_ref,
                  *, tm, tpb):
    i = pl.program_id(0)
    first = (i % tpb) == 0
    pos = lax.broadcasted_iota(jnp.int32, (tm, LANES), 0) + (i % tpb) * tm
    mixed = []
    for g, w in enumerate(POOL_WINDOWS):
        sl = slice(g * POOL_GROUP_DIM, (g + 1) * POOL_GROUP_DIM)
        u = u_ref[:, sl]
        halo = jnp.where(first, 0.0, uprev_ref[:, sl])
        ext = jnp.concatenate([halo, u], axis=0)
        span = 1
        while span < w:
            ext = ext + pltpu.roll(ext, span, 0)
            span *= 2
        cnt = jnp.minimum(pos + 1, w).astype(F32)
        diff = ext[POOL_HALO:, :] / cnt - u
        m = jnp.dot(diff.astype(BF16), poolw_ref[g], preferred_element_type=F32)
        mixed.append((m * pscale_ref[:, sl]).astype(BF16))
    pooled = jnp.concatenate(mixed, axis=1)
    bp = jnp.dot(pooled, wp_ref[...], preferred_element_type=F32)
    ba = jnp.dot(attn_ref[...], wa_ref[...], preferred_element_type=F32)
    merged = gt_ref[:, :D_MODEL] * bp + gt_ref[:, D_MODEL:] * ba
    y = jnp.dot(merged.astype(BF16), wo_ref[...], preferred_element_type=F32)
    x1 = x_ref[...] + gate1_ref[0] * y
    x1_ref[...] = x1
    h2 = _rms_modulate(x1, g2_ref[...], sh2_ref[0], sc2_ref[0])
    h2t_ref[...] = h2.T.astype(BF16)


def _merge(u_pool, attn, gates, x2, poolw, pscale, wp, wa, wo, gate1, g2, shift2, scale2,
           *, seq, tm):
    t = x2.shape[0]
    tpb = seq // tm
    row = lambda i: (i, 0)
    per_batch = lambda i: (i // tpb, 0, 0)
    halo_blocks = tm // POOL_HALO
    return pl.pallas_call(
        functools.partial(_merge_kernel, tm=tm, tpb=tpb),
        grid=(t // tm,),
        in_specs=[pl.BlockSpec((tm, POOL_WIDTH), row),
                  pl.BlockSpec((POOL_HALO, POOL_WIDTH),
                               lambda i: (jnp.maximum(i * halo_blocks - 1, 0), 0)),
                  pl.BlockSpec((tm, ATTN_PAD), row),
                  pl.BlockSpec((tm, 2 * D_MODEL), row),
                  pl.BlockSpec((tm, D_MODEL), row),
                  _const_spec((len(POOL_WINDOWS), POOL_GROUP_DIM, POOL_GROUP_DIM)),
                  _const_spec((1, POOL_WIDTH)),
                  _const_spec((POOL_WIDTH, D_MODEL)),
                  _const_spec((ATTN_PAD, D_MODEL)),
                  _const_spec((D_MODEL, D_MODEL)),
                  pl.BlockSpec((1, 1, D_MODEL), per_batch),
                  _const_spec((1, D_MODEL)),
                  pl.BlockSpec((1, 1, D_MODEL), per_batch),
                  pl.BlockSpec((1, 1, D_MODEL), per_batch)],
        out_specs=[pl.BlockSpec((tm, D_MODEL), row),
                   pl.BlockSpec((D_MODEL, tm), lambda i: (0, i))],
        out_shape=[jax.ShapeDtypeStruct((t, D_MODEL), F32),
                   jax.ShapeDtypeStruct((D_MODEL, t), BF16)],
        compiler_params=_params("arbitrary"),
        name="mixer_merge",
    )(u_pool, u_pool, attn, gates, x2, poolw, pscale, wp, wa, wo, gate1, g2, shift2, scale2)


def _extract_top16(streams):
    width = streams[0][0].shape[1]
    slot = lax.broadcasted_iota(jnp.int32, (PEER_TOPK, width), 0)

    def body(r, carry):
        out = []
        for (work, rank, vals), (_, index_f) in zip(carry, streams):
            m = jnp.max(work, axis=0, keepdims=True)
            idx = jnp.min(jnp.where(work == m, index_f, jnp.inf), axis=0, keepdims=True)
            hit = index_f == idx
            out.append((jnp.where(hit, -jnp.inf, work),
                        jnp.where(hit, lax.convert_element_type(r, F32), rank),
                        jnp.where(slot == r, m, vals)))
        return tuple(out)

    init = tuple((sc, jnp.full(sc.shape, NOT_MEMBER, F32), jnp.zeros((PEER_TOPK, width), F32))
                 for sc, _ in streams)
    return [(vals, rank) for _, rank, vals in lax.fori_loop(0, PEER_TOPK, body, init)]


def _batcher_pairs(n):
    pairs, p = [], 1
    while p < n:
        k = p
        while k >= 1:
            for j in range(k % p, n - k, 2 * k):
                for i in range(min(k, n - j - k)):
                    if (i + j) // (2 * p) == (i + j + k) // (2 * p):
                        pairs.append((i + j, i + j + k))
            k //= 2
        p *= 2
    return pairs


def _sorted_top16(regs):
    x = list(regs)

    def exchange(i, j):
        hi, lo = x[i], x[j]
        if lo is None:
            return
        if hi is None:
            x[i], x[j] = lo, None
        else:
            x[i], x[j] = jnp.maximum(hi, lo), jnp.minimum(hi, lo)

    for i, j in _batcher_pairs(PEER_TOPK):
        exchange(i, j)
    for shift in (4, 2, 1):
        other = [None if v is None else pltpu.roll(v, shift, 0) for v in x]
        for i in range(PEER_TOPK):
            mine, theirs = x[i], other[PEER_TOPK - 1 - i]
            x[i] = theirs if mine is None else mine if theirs is None else jnp.maximum(mine, theirs)
        dist = PEER_TOPK // 2
        while dist >= 1:
            for i in range(PEER_TOPK):
                if i & dist == 0:
                    exchange(i, i + dist)
            dist //= 2
    return x


def _route_kernel(h2t_ref, wqt_ref, keys_ref, r2_ref, e1_ref, n_ref, c_ref, qt_scr, *, tm):
    qt_scr[...] = jnp.dot(wqt_ref[...], h2t_ref[...], preferred_element_type=F32)
    half = PEER_TOPK // 2
    width = LANES

    def iota_f(rows):
        return lax.broadcasted_iota(jnp.int32, (rows, width), 0).astype(F32)

    tail = PEER_TOPK + (half - 1) * half
    slot = lax.broadcasted_iota(jnp.int32, (half, width), 0)

    def candidates(a, b):
        return jnp.concatenate([a[0:1, :] + b]
                               + [a[r:r + 1, :] + b[:half, :] for r in range(1, half)]
                               + [a[half:, :] + b[0:1, :]], axis=0)

    def staircase(chosen, a, b):
        e0 = jnp.exp(a - a[0:1, :])
        e1 = jnp.exp(b - b[0:1, :])
        n_lo = jnp.zeros((half, width), F32)
        pref_lo = jnp.zeros((half, width), F32)
        for r in range(half):
            lo = 0 if r == 0 else PEER_TOPK + (r - 1) * half
            cnt = PEER_TOPK if r == 0 else half
            grp = chosen[lo:lo + cnt, :]
            n_lo = jnp.where(slot == r, jnp.sum(grp, axis=0, keepdims=True), n_lo)
            pref_lo = jnp.where(slot == r, jnp.sum(grp * e1[:cnt, :], axis=0, keepdims=True),
                                pref_lo)
        n = jnp.concatenate([n_lo, chosen[tail:, :]], axis=0)
        pref = jnp.concatenate([pref_lo, chosen[tail:, :]], axis=0)
        return n, jnp.sum(e0 * pref, axis=0, keepdims=True)

    def emit(h, cols, s0, s1, a, b, rank2, n_dense, z):
        r2_ref[h, :, cols] = rank2
        e1_ref[h, :, cols] = jnp.exp(s1 - b[0:1, :])
        n_ref[h, :, cols] = n_dense
        c_ref[h, :, cols] = 0.5 * jnp.exp(s0 - a[0:1, :]) / z

    def per_head(h, carry):
        for cc in range(tm // width):
            route(h, slice(cc * width, (cc + 1) * width))
        return carry

    def route(h, cols):
        def half_scores(p):
            rows = pl.ds(pl.multiple_of((2 * h + p) * PEER_HALF, PEER_HALF), PEER_HALF)
            return jnp.dot(keys_ref[2 * h + p], qt_scr[rows, cols], precision=HIGHEST,
                           preferred_element_type=F32)

        s0 = half_scores(0)
        s1 = half_scores(1)

        pieces = PEER_NKEYS // SUBLANES
        s0v = [s0[v * SUBLANES:(v + 1) * SUBLANES, :] for v in range(pieces)]
        s1v = [s1[v * SUBLANES:(v + 1) * SUBLANES, :] for v in range(pieces)]
        a = _sorted_top16(s0v)
        b = _sorted_top16(s1v)
        sub = lax.broadcasted_iota(jnp.int32, (SUBLANES, width), 0)

        def stack(vals):
            out = vals[0]
            for j in range(1, SUBLANES):
                out = jnp.where(sub == j, vals[j], out)
            return out

        def total(x):
            return jnp.broadcast_to(jnp.sum(x, axis=0, keepdims=True), (SUBLANES, width))

        b_lo, b_hi, a_hi = stack(b[:half]), stack(b[half:]), stack(a[half:])
        cand = ([a[0] + b_lo, a[0] + b_hi] + [a[r] + b_lo for r in range(1, half)]
                + [a_hi + b[0]])
        top = _sorted_top16(cand + [None] * (PEER_TOPK - len(cand)))
        chosen = [jnp.where(c >= top[-1], 1.0, 0.0) for c in cand]
        n = ([total(chosen[0] + chosen[1])] + [total(chosen[1 + r]) for r in range(1, half)]
             + [total(jnp.where(sub == j, chosen[-1], 0.0)) for j in range(half)])
        z = total(sum(ch * jnp.exp(c - top[0]) for ch, c in zip(chosen, cand)))
        rank2, n_dense = [], []
        for v in range(pieces):
            r2, nd = jnp.full((SUBLANES, width), NOT_MEMBER, F32), jnp.zeros((SUBLANES, width), F32)
            for r in range(PEER_TOPK):
                r2 = jnp.where(s1v[v] == b[r], float(r), r2)
                nd = jnp.where(s0v[v] == a[r], n[r], nd)
            rank2.append(r2)
            n_dense.append(nd)
        emit(h, cols, s0, s1, a[0], b[0], jnp.concatenate(rank2, axis=0),
             jnp.concatenate(n_dense, axis=0), z[0:1, :])

        distinct = jnp.ones((SUBLANES, width), F32)
        for lst in (a, b, top):
            for r in range(PEER_TOPK - 1):
                distinct = jnp.where(lst[r] == lst[r + 1], 0.0, distinct)
        at_least = lambda regs, thr: total(sum(jnp.where(x >= thr, 1.0, 0.0) for x in regs))
        sixteen = ((at_least(s0v, a[-1]) == PEER_TOPK) & (at_least(s1v, b[-1]) == PEER_TOPK)
                   & (total(sum(chosen)) == PEER_TOPK))
        tie_free = jnp.min(jnp.where(sixteen, distinct, 0.0))

        @pl.when(tie_free < 0.5)
        def _exact():
            key_f = iota_f(PEER_NKEYS)
            pair_f = jnp.concatenate(
                [iota_f(PEER_TOPK)]
                + [iota_f(half) + float(r * PEER_TOPK) for r in range(1, half)]
                + [(iota_f(half) + float(half)) * float(PEER_TOPK)], axis=0)
            (a, rank1), (b, rank2) = _extract_top16([(s0, key_f), (s1, key_f)])
            ((_, pick),) = _extract_top16([(candidates(a, b), pair_f)])
            n, z = staircase(jnp.where(pick < NOT_MEMBER, 1.0, 0.0), a, b)
            n_dense = jnp.zeros((PEER_NKEYS, width), F32)
            for r in range(PEER_TOPK):
                n_dense = jnp.where(rank1 == float(r), n[r:r + 1, :], n_dense)
            emit(h, cols, s0, s1, a, b, rank2, n_dense, z)

    lax.fori_loop(0, PEER_HEADS, per_head, 0)


def _route(h2t, wqt, keys, *, tm):
    t = h2t.shape[1]
    dense = jax.ShapeDtypeStruct((PEER_HEADS, PEER_NKEYS, t), F32)
    dense_spec = pl.BlockSpec((PEER_HEADS, PEER_NKEYS, tm), lambda i: (0, 0, i))
    return pl.pallas_call(
        functools.partial(_route_kernel, tm=tm),
        grid=(t // tm,),
        in_specs=[pl.BlockSpec((D_MODEL, tm), lambda i: (0, i)),
                  _const_spec((2 * PEER_HEADS * PEER_HALF, D_MODEL)),
                  _const_spec((2 * PEER_HEADS, PEER_NKEYS, PEER_HALF))],
        out_specs=[dense_spec] * 4,
        out_shape=[dense] * 4,
        scratch_shapes=[pltpu.VMEM((2 * PEER_HEADS * PEER_HALF, tm), F32)],
        compiler_params=_params("arbitrary"),
        name="peer_route",
    )(h2t, wqt, keys)


def _experts_kernel(h2t_ref, u_ref, vt_ref, r2_ref, e1_ref, n_ref, c_ref, x1_ref, gate2_ref,
                    gf_ref, o_ref, acc_scr, act_a, act_b, p_a, p_b, *, tm, sub_blocks):
    s = pl.program_id(1)
    last = pl.num_programs(1) - 1

    @pl.when(s == 0)
    def _fill():
        acc_scr[...] = jnp.zeros_like(acc_scr)
        p_b[...] = jnp.zeros_like(p_b)
        act_a[...] = jnp.dot(u_ref[...], h2t_ref[...], preferred_element_type=F32)

    quad = 4 * SUBLANES

    def step(act_new, act_old, p_new, p_old):
        n_cc = tm // LANES
        n_kq = PEER_NKEYS // quad
        eb = act_new.shape[0]
        k_split = D_MODEL // MXU_DEPTH
        m_split = MATMUL_ROW_CHUNKS
        m1, k1 = eb // m_split, D_MODEL // k_split
        m2, k2 = D_MODEL // m_split, eb // k_split
        stride = 2 * n_cc * n_kq // (m_split * k_split)

        def matmul_pieces(slot):
            if slot % stride:
                return
            piece = slot // stride
            mi, ki = divmod(piece, k_split)
            u_rows = slice(mi * m1, (mi + 1) * m1)
            ks = slice(ki * k1, (ki + 1) * k1)
            part = jnp.dot(u_ref[u_rows, ks], h2t_ref[ks, :], preferred_element_type=F32)
            if ki == 0:
                act_new[u_rows, :] = part
            else:
                act_new[u_rows, :] += part
            d_rows = slice(mi * m2, (mi + 1) * m2)
            ks = slice(ki * k2, (ki + 1) * k2)
            acc_scr[d_rows, :] += jnp.dot(vt_ref[d_rows, ks], p_old[ks, :],
                                          preferred_element_type=F32)

        for cc in range(n_cc):
            cols = slice(cc * LANES, (cc + 1) * LANES)
            for kq in range(n_kq):
                block = cc * n_kq + kq
                matmul_pieces(2 * block)
                i2 = slice(kq * quad, (kq + 1) * quad)
                w = [jnp.zeros((quad, LANES), F32) for _ in range(sub_blocks)]
                for h in range(PEER_HEADS):
                    r2 = jnp.maximum(r2_ref[h, i2, cols], 0.0)
                    e1 = jnp.maximum(e1_ref[h, i2, cols], 0.0)
                    for jb in range(sub_blocks):
                        keep = r2 < n_ref[h, jb:jb + 1, cols]
                        w[jb] = w[jb] + jnp.where(keep, e1 * c_ref[h, jb:jb + 1, cols], 0.0)
                matmul_pieces(2 * block + 1)
                for jb in range(sub_blocks):
                    rows = slice(jb * PEER_NKEYS + kq * quad, jb * PEER_NKEYS + (kq + 1) * quad)
                    a = act_old[rows, cols]
                    p_new[rows, cols] = (w[jb] * (a * (1.0 + lax.erf(a * SQRT_HALF)))).astype(BF16)

    @pl.when((s % 2 == 0) & (s > 0))
    def _even():
        step(act_a, act_b, p_b, p_a)

    @pl.when((s % 2 == 1) & (s < last))
    def _odd():
        step(act_b, act_a, p_a, p_b)

    @pl.when(s == last)
    def _drain():
        acc_scr[...] += jnp.dot(vt_ref[...], p_b[...], preferred_element_type=F32)
        x2 = x1_ref[...] + gate2_ref[0] * acc_scr[...].T
        ms = jnp.mean(x2 * x2, axis=-1, keepdims=True)
        o_ref[...] = x2 * lax.rsqrt(ms + EPS) * gf_ref[...]


def _experts(h2t, u, v, r2, e1, n, c, x1, gate2, gf, *, seq, tm, sub_blocks):
    t = h2t.shape[1]
    tpb = seq // tm
    eb = sub_blocks * PEER_NKEYS
    n_eb = PEER_EXPERTS // eb
    assert n_eb % 2 == 0
    u_b = u.astype(BF16)
    vt_b = v.reshape(n_eb, eb, D_MODEL).transpose(0, 2, 1).astype(BF16)
    tok3 = lambda i, s: (0, 0, i)
    blk = lambda s, lag: jnp.clip(s - lag, 0, n_eb - 1)
    return pl.pallas_call(
        functools.partial(_experts_kernel, tm=tm, sub_blocks=sub_blocks),
        grid=(t // tm, n_eb + 2),
        in_specs=[pl.BlockSpec((D_MODEL, tm), lambda i, s: (0, i)),
                  pl.BlockSpec((eb, D_MODEL), lambda i, s: (blk(s, 0), 0)),
                  pl.BlockSpec((None, D_MODEL, eb), lambda i, s: (blk(s, 2), 0, 0)),
                  pl.BlockSpec((PEER_HEADS, PEER_NKEYS, tm), tok3),
                  pl.BlockSpec((PEER_HEADS, PEER_NKEYS, tm), tok3),
                  pl.BlockSpec((PEER_HEADS, sub_blocks, tm), lambda i, s: (0, blk(s, 1), i)),
                  pl.BlockSpec((PEER_HEADS, sub_blocks, tm), lambda i, s: (0, blk(s, 1), i)),
                  pl.BlockSpec((tm, D_MODEL), lambda i, s: (i, 0)),
                  pl.BlockSpec((1, 1, D_MODEL), lambda i, s: (i // tpb, 0, 0)),
                  pl.BlockSpec((1, D_MODEL), lambda i, s: (0, 0))],
        out_specs=pl.BlockSpec((tm, D_MODEL), lambda i, s: (i, 0)),
        out_shape=jax.ShapeDtypeStruct((t, D_MODEL), F32),
        scratch_shapes=[pltpu.VMEM((D_MODEL, tm), F32),
                        pltpu.VMEM((eb, tm), F32), pltpu.VMEM((eb, tm), F32),
                        pltpu.VMEM((eb, tm), BF16), pltpu.VMEM((eb, tm), BF16)],
        compiler_params=_params("arbitrary", "arbitrary"),
        name="peer_experts",
    )(h2t, u_b, vt_b, r2, e1, n, c, x1, gate2, gf)


def _pad_heads(w):
    d = w.shape[0]
    w = w.reshape(d, ATTN_HEADS, HEAD_DIM)
    return jnp.pad(w, ((0, 0), (0, 0), (0, HEAD_PAD - HEAD_DIM))).reshape(d, ATTN_PAD)


def _rope_tables(seq):
    inv = ROPE_THETA ** (-jnp.arange(ROT_HALF, dtype=F32) / ROT_HALF)
    ang = jnp.arange(seq).astype(F32)[:, None] * inv[None, :]
    cos, sin = jnp.cos(ang), jnp.sin(ang)
    z = lambda n: jnp.zeros((seq, n), F32)
    cos_t = jnp.concatenate([cos, cos, jnp.ones((seq, LANES - 2 * ROT_HALF), F32)], axis=1)
    sa_t = jnp.concatenate([-sin, z(LANES - ROT_HALF)], axis=1)
    sb_t = jnp.concatenate([z(ROT_HALF), sin, z(LANES - 2 * ROT_HALF)], axis=1)
    return cos_t, sa_t, sb_t


def _block_mean_rows(seq):
    r = jnp.arange(LANES)[:, None]
    s = jnp.arange(seq)[None, :]
    return jnp.where(r - BIAS_LANE0 == s // MOBA_BLOCK, 1.0 / MOBA_BLOCK, 0.0).astype(BF16)


def kernel(x, c, w_ada, b_ada, norm_mix_g, w_in, pool_w, pool_scale, w_branch_pool,
           w_branch_attn, w_out, norm_ffn_g, peer_wq, peer_sub_keys, peer_u, peer_v,
           norm_final_g):
    batch, seq, d = x.shape
    depth = w_ada.shape[0]
    assert d == D_MODEL and batch <= 8 and seq % 512 == 0
    assert seq // MOBA_BLOCK <= LANES - BIAS_LANE0
    assert depth == 1
    t = batch * seq
    x2 = x.reshape(t, d)
    c8 = jnp.pad(c, ((0, 8 - batch), (0, 0)))
    cos_t, sa_t, sb_t = _rope_tables(seq)
    bsel = _block_mean_rows(seq)
    row1 = lambda v: v.reshape(1, -1)
    per_batch = lambda v: v.reshape(batch, 1, d)

    for l in range(depth):
        mod = _ada(c8, w_ada[l], row1(b_ada[l]))[:batch]
        shift1, scale1, gate1, shift2, scale2, gate2 = [per_batch(m) for m in
                                                        jnp.split(mod, 6, axis=-1)]
        wl = w_in[l]
        w_p = jnp.concatenate(
            [wl[:, :POOL_WIDTH],
             _pad_heads(wl[:, POOL_WIDTH:POOL_WIDTH + ATTN_WIDTH]),
             _pad_heads(wl[:, POOL_WIDTH + ATTN_WIDTH:POOL_WIDTH + 2 * ATTN_WIDTH]),
             _pad_heads(wl[:, POOL_WIDTH + 2 * ATTN_WIDTH:POOL_WIDTH + 3 * ATTN_WIDTH]),
             wl[:, POOL_WIDTH + 3 * ATTN_WIDTH:]], axis=1).astype(BF16)
        u_pool, qp, kp, vp, gates = _inproj(x2, shift1, scale1, row1(norm_mix_g[l]), w_p,
                                            cos_t, sa_t, sb_t, seq=seq, tm=256)
        attn = _moba(qp, kp, vp, bsel, batch=batch, seq=seq, group=8, heads=2)
        wa_p = _pad_heads(w_branch_attn[l].T).T.astype(BF16)
        x1, h2t = _merge(u_pool, attn, gates, x2, pool_w[l].astype(BF16), row1(pool_scale[l]),
                         w_branch_pool[l].astype(BF16), wa_p, w_out[l].astype(BF16), gate1,
                         row1(norm_ffn_g[l]), shift2, scale2, seq=seq, tm=256)
        keys = peer_sub_keys[l].reshape(2 * PEER_HEADS, PEER_NKEYS, PEER_HALF)
        r2, e1, n, cden = _route(h2t, peer_wq[l].T.astype(BF16), keys, tm=256)
        x2 = _experts(h2t, peer_u[l], peer_v[l], r2, e1, n, cden,
                      x1, gate2, row1(norm_final_g), seq=seq, tm=512, sub_blocks=8)
    return x2.reshape(batch, seq, d)
```

```python
import functools
from typing import NamedTuple

import jax
import jax.numpy as jnp
import numpy as np
from jax import lax
from jax.experimental import pallas as pl
from jax.experimental.pallas import tpu as pltpu

F32 = jnp.float32
BF16 = jnp.bfloat16
HIGHEST = lax.Precision.HIGHEST

D_MODEL = 1024
POOL_WINDOWS = (2, 4, 8, 16)
POOL_WIDTH = 512
POOL_GROUP_DIM = 128
POOL_HALO = 16
ATTN_HEADS = 8
HEAD_DIM = 64
ATTN_WIDTH = 512
MOBA_BLOCK = 256
MOBA_TOPK = 3
ROPE_THETA = 500000.0
ROT_HALF = 8
PEER_HEADS = 8
PEER_NKEYS = 128
PEER_EXPERTS = PEER_NKEYS * PEER_NKEYS
PEER_HALF = 128
PEER_TOPK = 16
EPS = 1e-6

LANES = 128
SUBLANES = 8
MXU_DEPTH = 256
MATMUL_ROW_CHUNKS = 4
HEAD_PAD = LANES
ATTN_PAD = ATTN_HEADS * HEAD_PAD
BIAS_LANE0 = HEAD_DIM
ONES_LANE = HEAD_DIM
MASK_BIG = 2.0 ** 100
CAUSAL_NEG = -1e30
NOT_MEMBER = 255.0
SQRT_HALF = float(np.sqrt(0.5).astype(np.float32))
VMEM_LIMIT = 56 * 1024 * 1024

IN_PROJ_PAD = POOL_WIDTH + 3 * ATTN_PAD + 2 * D_MODEL
Q_OFF = POOL_WIDTH
K_OFF = Q_OFF + ATTN_PAD
V_OFF = K_OFF + ATTN_PAD
G_OFF = V_OFF + ATTN_PAD


class _Tiles(NamedTuple):
    proj_tokens: int = 512
    moba_group: int = 8
    moba_heads: int = 2
    route_tokens: int = 256
    expert_tokens: int = 512
    expert_sub_blocks: int = 8


TILES = _Tiles()


def _params(*semantics):
    return pltpu.CompilerParams(dimension_semantics=semantics, vmem_limit_bytes=VMEM_LIMIT)


def _const_spec(shape):
    nd = len(shape)
    return pl.BlockSpec(shape, lambda *_: (0,) * nd, pipeline_mode=pl.Buffered(1))


def _rms_modulate(x, g, shift, scale):
    ms = jnp.mean(x * x, axis=-1, keepdims=True)
    return (x * lax.rsqrt(ms + EPS) * g) * (1.0 + scale) + shift


def _ada_kernel(c_ref, w_ref, b_ref, o_ref):
    c = c_ref[...]
    act = c / (1.0 + jnp.exp(-c))
    o_ref[...] = jnp.dot(act, w_ref[...], precision=HIGHEST,
                         preferred_element_type=F32) + b_ref[...]


def _ada(c8, w, b):
    n = w.shape[1]
    tn = 1536
    return pl.pallas_call(
        _ada_kernel,
        grid=(n // tn,),
        in_specs=[pl.BlockSpec((8, D_MODEL), lambda j: (0, 0)),
                  pl.BlockSpec((D_MODEL, tn), lambda j: (0, j)),
                  pl.BlockSpec((1, tn), lambda j: (0, j))],
        out_specs=pl.BlockSpec((8, tn), lambda j: (0, j)),
        out_shape=jax.ShapeDtypeStruct((8, n), F32),
        compiler_params=_params("arbitrary"),
        name="ada_mod",
    )(c8, w, b)


def _inproj_kernel(x_ref, sh_ref, sc_ref, g_ref, w_ref, cos_ref, sa_ref, sb_ref,
                   u_ref, q_ref, k_ref, v_ref, gt_ref):
    h = _rms_modulate(x_ref[...], g_ref[...], sh_ref[0], sc_ref[0]).astype(BF16)

    def proj(off, width):
        return jnp.dot(h, w_ref[:, off:off + width], preferred_element_type=F32)

    u_ref[...] = proj(0, POOL_WIDTH)
    cos, sa, sb = cos_ref[...], sa_ref[...], sb_ref[...]

    def rope(t):
        return t * cos + pltpu.roll(t, LANES - ROT_HALF, 1) * sa + pltpu.roll(t, ROT_HALF, 1) * sb

    for hd in range(ATTN_HEADS):
        sl = slice(hd * HEAD_PAD, (hd + 1) * HEAD_PAD)
        q = rope(proj(Q_OFF + hd * HEAD_PAD, HEAD_PAD))
        q_ref[:, sl] = (q * (HEAD_DIM ** -0.5)).astype(BF16)
        k_ref[:, sl] = rope(proj(K_OFF + hd * HEAD_PAD, HEAD_PAD)).astype(BF16)
    v_ref[...] = proj(V_OFF, ATTN_PAD).astype(BF16)
    gt_ref[...] = 1.0 / (1.0 + jnp.exp(-proj(G_OFF, 2 * D_MODEL)))


def _inproj(x2, shift, scale, g, w_p, cos_t, sa_t, sb_t, *, seq, tm):
    t = x2.shape[0]
    tpb = seq // tm
    row = lambda i: (i, 0)
    per_batch = lambda i: (i // tpb, 0, 0)
    per_pos = lambda i: (i % tpb, 0)
    return pl.pallas_call(
        _inproj_kernel,
        grid=(t // tm,),
        in_specs=[pl.BlockSpec((tm, D_MODEL), row),
                  pl.BlockSpec((1, 1, D_MODEL), per_batch),
                  pl.BlockSpec((1, 1, D_MODEL), per_batch),
                  _const_spec((1, D_MODEL)),
                  _const_spec((D_MODEL, IN_PROJ_PAD)),
                  pl.BlockSpec((tm, LANES), per_pos),
                  pl.BlockSpec((tm, LANES), per_pos),
                  pl.BlockSpec((tm, LANES), per_pos)],
        out_specs=[pl.BlockSpec((tm, POOL_WIDTH), row),
                   pl.BlockSpec((tm, ATTN_PAD), row),
                   pl.BlockSpec((tm, ATTN_PAD), row),
                   pl.BlockSpec((tm, ATTN_PAD), row),
                   pl.BlockSpec((tm, 2 * D_MODEL), row)],
        out_shape=[jax.ShapeDtypeStruct((t, POOL_WIDTH), F32),
                   jax.ShapeDtypeStruct((t, ATTN_PAD), BF16),
                   jax.ShapeDtypeStruct((t, ATTN_PAD), BF16),
                   jax.ShapeDtypeStruct((t, ATTN_PAD), BF16),
                   jax.ShapeDtypeStruct((t, 2 * D_MODEL), F32)],
        compiler_params=_params("arbitrary"),
        name="in_proj",
    )(x2, shift, scale, g, w_p, cos_t, sa_t, sb_t)


def _moba_kernel(q_ref, k_ref, v_ref, bsel_ref, o_ref, qa_scr, ka_scr, va_scr, *, n_blocks,
                 group, heads):
    qi = pl.program_id(2)
    blk = MOBA_BLOCK
    lane = lax.broadcasted_iota(jnp.int32, (blk, LANES), 1)
    lane_f = lane.astype(F32)
    head_lanes = [slice(i * HEAD_PAD, (i + 1) * HEAD_PAD) for i in range(heads)]
    nt = (((1,), (1,)), ((), ()))

    @pl.when(qi == 0)
    def _prepare():
        kms = [jnp.dot(bsel_ref[...], k_ref[:, hl], preferred_element_type=F32)
               for hl in head_lanes]

        def per_block(nb, carry):
            rows = pl.ds(pl.multiple_of(nb * blk, blk), blk)
            valid = (lane >= BIAS_LANE0) & (lane < BIAS_LANE0 + nb)
            bias_lanes = (lane >= BIAS_LANE0) & (lane < BIAS_LANE0 + n_blocks)
            onehot = jnp.where(lane == BIAS_LANE0 + nb, 1.0, 0.0)
            qs = [q_ref[rows, hl].astype(F32) for hl in head_lanes]
            works = [jnp.where(valid, lax.dot_general(q, km, nt, precision=HIGHEST,
                                                      preferred_element_type=F32), -jnp.inf)
                     for q, km in zip(qs, kms)]
            chosen = [jnp.zeros((blk, LANES), F32) for _ in head_lanes]
            for _ in range(MOBA_TOPK):
                ms = [jnp.max(w, axis=1, keepdims=True) for w in works]
                idxs = [jnp.min(jnp.where(w == m, lane_f, 2.0 * LANES), axis=1, keepdims=True)
                        for w, m in zip(works, ms)]
                hits = [(lane_f == idx) & valid for idx in idxs]
                chosen = [jnp.where(hit, 1.0, ch) for hit, ch in zip(hits, chosen)]
                works = [jnp.where(hit, -jnp.inf, w) for hit, w in zip(hits, works)]
            for hl, q, ch in zip(head_lanes, qs, chosen):
                masked = bias_lanes & (ch == 0.0)
                qa_scr[rows, hl] = jnp.where(masked, -MASK_BIG, q).astype(BF16)
                ka_scr[rows, hl] = jnp.where(lane < HEAD_DIM, k_ref[rows, hl].astype(F32),
                                             onehot).astype(BF16)
                va_scr[rows, hl] = jnp.where(lane == ONES_LANE, 1.0,
                                             v_ref[rows, hl].astype(F32)).astype(BF16)
            return carry

        lax.fori_loop(0, n_blocks, per_block, 0)

    own = pl.ds(pl.multiple_of(qi * blk, blk), blk)
    r_i = lax.broadcasted_iota(jnp.int32, (blk, blk), 0)
    c_i = lax.broadcasted_iota(jnp.int32, (blk, blk), 1)
    qas = [qa_scr[own, hl] for hl in head_lanes]
    own_s = [jnp.where(c_i <= r_i,
                       lax.dot_general(jnp.where(lane < HEAD_DIM, qa, jnp.zeros_like(qa)),
                                       ka_scr[own, hl], nt, preferred_element_type=F32),
                       CAUSAL_NEG) for hl, qa in zip(head_lanes, qas)]
    own_m = [jnp.max(s, axis=1, keepdims=True) for s in own_s]
    init = [(m0, jnp.dot(jnp.exp(s - m0).astype(BF16), va_scr[own, hl],
                         preferred_element_type=F32))
            for hl, s, m0 in zip(head_lanes, own_s, own_m)]

    span = group * blk

    def past(j, carry):
        rows = pl.ds(pl.multiple_of(j * span, span), span)
        scores = [lax.dot_general(qa, ka_scr[rows, hl], nt, preferred_element_type=F32)
                  for hl, qa in zip(head_lanes, qas)]
        m_new = [jnp.maximum(m, jnp.max(sj, axis=1, keepdims=True))
                 for sj, (m, _) in zip(scores, carry)]
        out = []
        for hl, sj, mn, (m, acc) in zip(head_lanes, scores, m_new, carry):
            pj = jnp.exp(sj - mn)
            acc = acc * jnp.exp(m - mn) + jnp.dot(pj.astype(BF16), va_scr[rows, hl],
                                                  preferred_element_type=F32)
            out.append((mn, acc))
        return tuple(out)

    final = lax.fori_loop(0, (qi + group - 1) // group, past, tuple(init))
    for hl, (_, acc) in zip(head_lanes, final):
        o_ref[:, hl] = (acc / acc[:, ONES_LANE:ONES_LANE + 1]).astype(BF16)


def _moba(qp, kp, vp, bsel, *, batch, seq, group, heads):
    n_blocks = seq // MOBA_BLOCK
    assert n_blocks % group == 0 and ATTN_HEADS % heads == 0
    width = heads * HEAD_PAD
    head = lambda b, h, i: (b, h)
    return pl.pallas_call(
        functools.partial(_moba_kernel, n_blocks=n_blocks, group=group, heads=heads),
        grid=(batch, ATTN_HEADS // heads, n_blocks),
        in_specs=[pl.BlockSpec((seq, width), head),
                  pl.BlockSpec((seq, width), head),
                  pl.BlockSpec((seq, width), head),
                  _const_spec((LANES, seq))],
        out_specs=pl.BlockSpec((MOBA_BLOCK, width), lambda b, h, i: (b * n_blocks + i, h)),
        out_shape=jax.ShapeDtypeStruct((batch * seq, ATTN_PAD), BF16),
        scratch_shapes=[pltpu.VMEM((seq, width), BF16)] * 3,
        compiler_params=_params("arbitrary", "arbitrary", "arbitrary"),
        name="moba_attn",
    )(qp, kp, vp, bsel)


def _merge_kernel(u_ref, uprev_ref, attn_ref, gt_ref, x_ref, poolw_ref, pscale_ref, wp_ref,
                  wa_ref, wo_ref, gate1_ref, g2_ref, sh2_ref, sc2_ref, x1_ref, h2t_ref,
                  *, tm, tpb):
    i = pl.program_id(0)
    first = (i % tpb) == 0
    pos = lax.broadcasted_iota(jnp.int32, (tm, LANES), 0) + (i % tpb) * tm
    mixed = []
    for g, w in enumerate(POOL_WINDOWS):
        sl = slice(g * POOL_GROUP_DIM, (g + 1) * POOL_GROUP_DIM)
        u = u_ref[:, sl]
        halo = jnp.where(first, 0.0, uprev_ref[:, sl])
        ext = jnp.concatenate([halo, u], axis=0)
        span = 1
        while span < w:
            ext = ext + pltpu.roll(ext, span, 0)
            span *= 2
        cnt = jnp.minimum(pos + 1, w).astype(F32)
        diff = ext[POOL_HALO:, :] / cnt - u
        m = jnp.dot(diff.astype(BF16), poolw_ref[g], preferred_element_type=F32)
        mixed.append((m * pscale_ref[:, sl]).astype(BF16))
    pooled = jnp.concatenate(mixed, axis=1)
    bp = jnp.dot(pooled, wp_ref[...], preferred_element_type=F32)
    ba = jnp.dot(attn_ref[...], wa_ref[...], preferred_element_type=F32)
    merged = gt_ref[:, :D_MODEL] * bp + gt_ref[:, D_MODEL:] * ba
    y = jnp.dot(merged.astype(BF16), wo_ref[...], preferred_element_type=F32)
    x1 = x_ref[...] + gate1_ref[0] * y
    x1_ref[...] = x1
    h2 = _rms_modulate(x1, g2_ref[...], sh2_ref[0], sc2_ref[0])
    h2t_ref[...] = h2.T.astype(BF16)


def _merge(u_pool, attn, gates, x2, poolw, pscale, wp, wa, wo, gate1, g2, shift2, scale2,
           *, seq, tm):
    t = x2.shape[0]
    tpb = seq // tm
    row = lambda i: (i, 0)
    per_batch = lambda i: (i // tpb, 0, 0)
    halo_blocks = tm // POOL_HALO
    return pl.pallas_call(
        functools.partial(_merge_kernel, tm=tm, tpb=tpb),
        grid=(t // tm,),
        in_specs=[pl.BlockSpec((tm, POOL_WIDTH), row),
                  pl.BlockSpec((POOL_HALO, POOL_WIDTH),
                               lambda i: (jnp.maximum(i * halo_blocks - 1, 0), 0)),
                  pl.BlockSpec((tm, ATTN_PAD), row),
                  pl.BlockSpec((tm, 2 * D_MODEL), row),
                  pl.BlockSpec((tm, D_MODEL), row),
                  _const_spec((len(POOL_WINDOWS), POOL_GROUP_DIM, POOL_GROUP_DIM)),
                  _const_spec((1, POOL_WIDTH)),
                  _const_spec((POOL_WIDTH, D_MODEL)),
                  _const_spec((ATTN_PAD, D_MODEL)),
                  _const_spec((D_MODEL, D_MODEL)),
                  pl.BlockSpec((1, 1, D_MODEL), per_batch),
                  _const_spec((1, D_MODEL)),
                  pl.BlockSpec((1, 1, D_MODEL), per_batch),
                  pl.BlockSpec((1, 1, D_MODEL), per_batch)],
        out_specs=[pl.BlockSpec((tm, D_MODEL), row),
                   pl.BlockSpec((D_MODEL, tm), lambda i: (0, i))],
        out_shape=[jax.ShapeDtypeStruct((t, D_MODEL), F32),
                   jax.ShapeDtypeStruct((D_MODEL, t), BF16)],
        compiler_params=_params("arbitrary"),
        name="mixer_merge",
    )(u_pool, u_pool, attn, gates, x2, poolw, pscale, wp, wa, wo, gate1, g2, shift2, scale2)


def _extract_top16(streams):
    width = streams[0][0].shape[1]
    slot = lax.broadcasted_iota(jnp.int32, (PEER_TOPK, width), 0)

    def body(r, carry):
        out = []
        for (work, rank, vals), (_, index_f) in zip(carry, streams):
            m = jnp.max(work, axis=0, keepdims=True)
            idx = jnp.min(jnp.where(work == m, index_f, jnp.inf), axis=0, keepdims=True)
            hit = index_f == idx
            out.append((jnp.where(hit, -jnp.inf, work),
                        jnp.where(hit, lax.convert_element_type(r, F32), rank),
                        jnp.where(slot == r, m, vals)))
        return tuple(out)

    init = tuple((sc, jnp.full(sc.shape, NOT_MEMBER, F32), jnp.zeros((PEER_TOPK, width), F32))
                 for sc, _ in streams)
    return [(vals, rank) for _, rank, vals in lax.fori_loop(0, PEER_TOPK, body, init)]


def _batcher_pairs(n):
    pairs, p = [], 1
    while p < n:
        k = p
        while k >= 1:
            for j in range(k % p, n - k, 2 * k):
                for i in range(min(k, n - j - k)):
                    if (i + j) // (2 * p) == (i + j + k) // (2 * p):
                        pairs.append((i + j, i + j + k))
            k //= 2
        p *= 2
    return pairs


def _sorted_top16(regs):
    x = list(regs)

    def exchange(i, j):
        hi, lo = x[i], x[j]
        if lo is None:
            return
        if hi is None:
            x[i], x[j] = lo, None
        else:
            x[i], x[j] = jnp.maximum(hi, lo), jnp.minimum(hi, lo)

    for i, j in _batcher_pairs(PEER_TOPK):
        exchange(i, j)
    for shift in (4, 2, 1):
        other = [None if v is None else pltpu.roll(v, shift, 0) for v in x]
        for i in range(PEER_TOPK):
            mine, theirs = x[i], other[PEER_TOPK - 1 - i]
            x[i] = theirs if mine is None else mine if theirs is None else jnp.maximum(mine, theirs)
        dist = PEER_TOPK // 2
        while dist >= 1:
            for i in range(PEER_TOPK):
                if i & dist == 0:
                    exchange(i, i + dist)
            dist //= 2
    return x


def _route_kernel(h2t_ref, wqt_ref, keys_ref, r2_ref, e1_ref, n_ref, c_ref, qt_scr, *, tm):
    qt_scr[...] = jnp.dot(wqt_ref[...], h2t_ref[...], preferred_element_type=F32)
    half = PEER_TOPK // 2
    width = 2 * LANES

    def iota_f(rows):
        return lax.broadcasted_iota(jnp.int32, (rows, width), 0).astype(F32)

    tail = PEER_TOPK + (half - 1) * half
    slot = lax.broadcasted_iota(jnp.int32, (half, width), 0)

    def candidates(a, b):
        return jnp.concatenate([a[0:1, :] + b]
                               + [a[r:r + 1, :] + b[:half, :] for r in range(1, half)]
                               + [a[half:, :] + b[0:1, :]], axis=0)

    def staircase(chosen, a, b):
        e0 = jnp.exp(a - a[0:1, :])
        e1 = jnp.exp(b - b[0:1, :])
        n_lo = jnp.zeros((half, width), F32)
        pref_lo = jnp.zeros((half, width), F32)
        for r in range(half):
            lo = 0 if r == 0 else PEER_TOPK + (r - 1) * half
            cnt = PEER_TOPK if r == 0 else half
            grp = chosen[lo:lo + cnt, :]
            n_lo = jnp.where(slot == r, jnp.sum(grp, axis=0, keepdims=True), n_lo)
            pref_lo = jnp.where(slot == r, jnp.sum(grp * e1[:cnt, :], axis=0, keepdims=True),
                                pref_lo)
        n = jnp.concatenate([n_lo, chosen[tail:, :]], axis=0)
        pref = jnp.concatenate([pref_lo, chosen[tail:, :]], axis=0)
        return n, jnp.sum(e0 * pref, axis=0, keepdims=True)

    def emit(h, cols, s0, s1, a, b, rank2, n_dense, z):
        r2_ref[h, :, cols] = rank2
        e1_ref[h, :, cols] = jnp.exp(s1 - b[0:1, :])
        n_ref[h, :, cols] = n_dense
        c_ref[h, :, cols] = 0.5 * jnp.exp(s0 - a[0:1, :]) / z

    def per_head(h, carry):
        for cc in range(tm // width):
            route(h, slice(cc * width, (cc + 1) * width))
        return carry

    def route(h, cols):
        def half_scores(p):
            rows = pl.ds(pl.multiple_of((2 * h + p) * PEER_HALF, PEER_HALF), PEER_HALF)
            return jnp.dot(keys_ref[2 * h + p], qt_scr[rows, cols], precision=HIGHEST,
                           preferred_element_type=F32)

        s0 = half_scores(0)
        s1 = half_scores(1)

        pieces = PEER_NKEYS // SUBLANES
        s0v = [s0[v * SUBLANES:(v + 1) * SUBLANES, :] for v in range(pieces)]
        s1v = [s1[v * SUBLANES:(v + 1) * SUBLANES, :] for v in range(pieces)]
        a = _sorted_top16(s0v)
        b = _sorted_top16(s1v)
        sub = lax.broadcasted_iota(jnp.int32, (SUBLANES, width), 0)

        def stack(vals):
            out = vals[0]
            for j in range(1, SUBLANES):
                out = jnp.where(sub == j, vals[j], out)
            return out

        def total(x):
            return jnp.broadcast_to(jnp.sum(x, axis=0, keepdims=True), (SUBLANES, width))

        b_lo, b_hi, a_hi = stack(b[:half]), stack(b[half:]), stack(a[half:])
        cand = ([a[0] + b_lo, a[0] + b_hi] + [a[r] + b_lo for r in range(1, half)]
                + [a_hi + b[0]])
        top = _sorted_top16(cand + [None] * (PEER_TOPK - len(cand)))
        chosen = [jnp.where(c >= top[-1], 1.0, 0.0) for c in cand]
        n = ([total(chosen[0] + chosen[1])] + [total(chosen[1 + r]) for r in range(1, half)]
             + [total(jnp.where(sub == j, chosen[-1], 0.0)) for j in range(half)])
        z = total(sum(ch * jnp.exp(c - top[0]) for ch, c in zip(chosen, cand)))
        rank2, n_dense = [], []
        for v in range(pieces):
            r2, nd = jnp.full((SUBLANES, width), NOT_MEMBER, F32), jnp.zeros((SUBLANES, width), F32)
            for r in range(PEER_TOPK):
                r2 = jnp.where(s1v[v] == b[r], float(r), r2)
                nd = jnp.where(s0v[v] == a[r], n[r], nd)
            rank2.append(r2)
            n_dense.append(nd)
        emit(h, cols, s0, s1, a[0], b[0], jnp.concatenate(rank2, axis=0),
             jnp.concatenate(n_dense, axis=0), z[0:1, :])

        distinct = jnp.ones((SUBLANES, width), F32)
        for lst in (a, b, top):
            for r in range(PEER_TOPK - 1):
                distinct = jnp.where(lst[r] == lst[r + 1], 0.0, distinct)
        at_least = lambda regs, thr: total(sum(jnp.where(x >= thr, 1.0, 0.0) for x in regs))
        sixteen = ((at_least(s0v, a[-1]) == PEER_TOPK) & (at_least(s1v, b[-1]) == PEER_TOPK)
                   & (total(sum(chosen)) == PEER_TOPK))
        tie_free = jnp.min(jnp.where(sixteen, distinct, 0.0))

        @pl.when(tie_free < 0.5)
        def _exact():
            key_f = iota_f(PEER_NKEYS)
            pair_f = jnp.concatenate(
                [iota_f(PEER_TOPK)]
                + [iota_f(half) + float(r * PEER_TOPK) for r in range(1, half)]
                + [(iota_f(half) + float(half)) * float(PEER_TOPK)], axis=0)
            (a, rank1), (b, rank2) = _extract_top16([(s0, key_f), (s1, key_f)])
            ((_, pick),) = _extract_top16([(candidates(a, b), pair_f)])
            n, z = staircase(jnp.where(pick < NOT_MEMBER, 1.0, 0.0), a, b)
            n_dense = jnp.zeros((PEER_NKEYS, width), F32)
            for r in range(PEER_TOPK):
                n_dense = jnp.where(rank1 == float(r), n[r:r + 1, :], n_dense)
            emit(h, cols, s0, s1, a, b, rank2, n_dense, z)

    lax.fori_loop(0, PEER_HEADS, per_head, 0)


def _route(h2t, wqt, keys, *, tm):
    t = h2t.shape[1]
    dense = jax.ShapeDtypeStruct((PEER_HEADS, PEER_NKEYS, t), F32)
    dense_spec = pl.BlockSpec((PEER_HEADS, PEER_NKEYS, tm), lambda i: (0, 0, i))
    return pl.pallas_call(
        functools.partial(_route_kernel, tm=tm),
        grid=(t // tm,),
        in_specs=[pl.BlockSpec((D_MODEL, tm), lambda i: (0, i)),
                  _const_spec((2 * PEER_HEADS * PEER_HALF, D_MODEL)),
                  _const_spec((2 * PEER_HEADS, PEER_NKEYS, PEER_HALF))],
        out_specs=[dense_spec] * 4,
        out_shape=[dense] * 4,
        scratch_shapes=[pltpu.VMEM((2 * PEER_HEADS * PEER_HALF, tm), F32)],
        compiler_params=_params("arbitrary"),
        name="peer_route",
    )(h2t, wqt, keys)


def _experts_kernel(h2t_ref, u_ref, vt_ref, r2_ref, e1_ref, n_ref, c_ref, x1_ref, gate2_ref,
                    gf_ref, o_ref, acc_scr, act_a, act_b, p_a, p_b, *, tm, sub_blocks):
    s = pl.program_id(1)
    last = pl.num_programs(1) - 1

    @pl.when(s == 0)
    def _fill():
        acc_scr[...] = jnp.zeros_like(acc_scr)
        p_b[...] = jnp.zeros_like(p_b)
        act_a[...] = jnp.dot(u_ref[...], h2t_ref[...], preferred_element_type=F32)

    quad = 4 * SUBLANES

    def step(act_new, act_old, p_new, p_old):
        n_cc = tm // LANES
        n_kq = PEER_NKEYS // quad
        eb = act_new.shape[0]
        k_split = D_MODEL // MXU_DEPTH
        m_split = MATMUL_ROW_CHUNKS
        m1, k1 = eb // m_split, D_MODEL // k_split
        m2, k2 = D_MODEL // m_split, eb // k_split
        stride = 2 * n_cc * n_kq // (m_split * k_split)

        def matmul_pieces(slot):
            if slot % stride:
                return
            piece = slot // stride
            mi, ki = divmod(piece, k_split)
            u_rows = slice(mi * m1, (mi + 1) * m1)
            ks = slice(ki * k1, (ki + 1) * k1)
            part = jnp.dot(u_ref[u_rows, ks], h2t_ref[ks, :], preferred_element_type=F32)
            if ki == 0:
                act_new[u_rows, :] = part
            else:
                act_new[u_rows, :] += part
            d_rows = slice(mi * m2, (mi + 1) * m2)
            ks = slice(ki * k2, (ki + 1) * k2)
            acc_scr[d_rows, :] += jnp.dot(vt_ref[d_rows, ks], p_old[ks, :],
                                          preferred_element_type=F32)

        for cc in range(n_cc):
            cols = slice(cc * LANES, (cc + 1) * LANES)
            for kq in range(n_kq):
                block = cc * n_kq + kq
                matmul_pieces(2 * block)
                i2 = slice(kq * quad, (kq + 1) * quad)
                w = [jnp.zeros((quad, LANES), F32) for _ in range(sub_blocks)]
                for h in range(PEER_HEADS):
                    r2 = r2_ref[h, i2, cols]
                    e1 = e1_ref[h, i2, cols]
                    for jb in range(sub_blocks):
                        keep = r2 < n_ref[h, jb:jb + 1, cols]
                        w[jb] = w[jb] + jnp.where(keep, e1 * c_ref[h, jb:jb + 1, cols], 0.0)
                matmul_pieces(2 * block + 1)
                for jb in range(sub_blocks):
                    rows = slice(jb * PEER_NKEYS + kq * quad, jb * PEER_NKEYS + (kq + 1) * quad)
                    a = act_old[rows, cols]
                    p_new[rows, cols] = (w[jb] * (a * (1.0 + lax.erf(a * SQRT_HALF)))).astype(BF16)

    @pl.when((s % 2 == 0) & (s > 0))
    def _even():
        step(act_a, act_b, p_b, p_a)

    @pl.when((s % 2 == 1) & (s < last))
    def _odd():
        step(act_b, act_a, p_a, p_b)

    @pl.when(s == last)
    def _drain():
        acc_scr[...] += jnp.dot(vt_ref[...], p_b[...], preferred_element_type=F32)
        x2 = x1_ref[...] + gate2_ref[0] * acc_scr[...].T
        ms = jnp.mean(x2 * x2, axis=-1, keepdims=True)
        o_ref[...] = x2 * lax.rsqrt(ms + EPS) * gf_ref[...]


def _experts(h2t, u, v, r2, e1, n, c, x1, gate2, gf, *, seq, tm, sub_blocks):
    t = h2t.shape[1]
    tpb = seq // tm
    eb = sub_blocks * PEER_NKEYS
    n_eb = PEER_EXPERTS // eb
    assert n_eb % 2 == 0
    u_b = u.astype(BF16)
    vt_b = v.reshape(n_eb, eb, D_MODEL).transpose(0, 2, 1).astype(BF16)
    tok3 = lambda i, s: (0, 0, i)
    blk = lambda s, lag: jnp.clip(s - lag, 0, n_eb - 1)
    return pl.pallas_call(
        functools.partial(_experts_kernel, tm=tm, sub_blocks=sub_blocks),
        grid=(t // tm, n_eb + 2),
        in_specs=[pl.BlockSpec((D_MODEL, tm), lambda i, s: (0, i)),
                  pl.BlockSpec((eb, D_MODEL), lambda i, s: (blk(s, 0), 0)),
                  pl.BlockSpec((None, D_MODEL, eb), lambda i, s: (blk(s, 2), 0, 0)),
                  pl.BlockSpec((PEER_HEADS, PEER_NKEYS, tm), tok3),
                  pl.BlockSpec((PEER_HEADS, PEER_NKEYS, tm), tok3),
                  pl.BlockSpec((PEER_HEADS, sub_blocks, tm), lambda i, s: (0, blk(s, 1), i)),
                  pl.BlockSpec((PEER_HEADS, sub_blocks, tm), lambda i, s: (0, blk(s, 1), i)),
                  pl.BlockSpec((tm, D_MODEL), lambda i, s: (i, 0)),
                  pl.BlockSpec((1, 1, D_MODEL), lambda i, s: (i // tpb, 0, 0)),
                  pl.BlockSpec((1, D_MODEL), lambda i, s: (0, 0))],
        out_specs=pl.BlockSpec((tm, D_MODEL), lambda i, s: (i, 0)),
        out_shape=jax.ShapeDtypeStruct((t, D_MODEL), F32),
        scratch_shapes=[pltpu.VMEM((D_MODEL, tm), F32),
                        pltpu.VMEM((eb, tm), F32), pltpu.VMEM((eb, tm), F32),
                        pltpu.VMEM((eb, tm), BF16), pltpu.VMEM((eb, tm), BF16)],
        compiler_params=_params("arbitrary", "arbitrary"),
        name="peer_experts",
    )(h2t, u_b, vt_b, r2, e1, n, c, x1, gate2, gf)


def _pad_heads(w):
    d = w.shape[0]
    w = w.reshape(d, ATTN_HEADS, HEAD_DIM)
    return jnp.pad(w, ((0, 0), (0, 0), (0, HEAD_PAD - HEAD_DIM))).reshape(d, ATTN_PAD)


def _rope_tables(seq):
    inv = ROPE_THETA ** (-jnp.arange(ROT_HALF, dtype=F32) / ROT_HALF)
    ang = jnp.arange(seq).astype(F32)[:, None] * inv[None, :]
    cos, sin = jnp.cos(ang), jnp.sin(ang)
    z = lambda n: jnp.zeros((seq, n), F32)
    cos_t = jnp.concatenate([cos, cos, jnp.ones((seq, LANES - 2 * ROT_HALF), F32)], axis=1)
    sa_t = jnp.concatenate([-sin, z(LANES - ROT_HALF)], axis=1)
    sb_t = jnp.concatenate([z(ROT_HALF), sin, z(LANES - 2 * ROT_HALF)], axis=1)
    return cos_t, sa_t, sb_t


def _block_mean_rows(seq):
    r = jnp.arange(LANES)[:, None]
    s = jnp.arange(seq)[None, :]
    return jnp.where(r - BIAS_LANE0 == s // MOBA_BLOCK, 1.0 / MOBA_BLOCK, 0.0).astype(BF16)


def kernel(x, c, w_ada, b_ada, norm_mix_g, w_in, pool_w, pool_scale, w_branch_pool,
           w_branch_attn, w_out, norm_ffn_g, peer_wq, peer_sub_keys, peer_u, peer_v,
           norm_final_g):
    batch, seq, d = x.shape
    depth = w_ada.shape[0]
    assert d == D_MODEL and batch <= 8
    assert seq % max(TILES.proj_tokens, TILES.expert_tokens, TILES.route_tokens) == 0
    assert seq // MOBA_BLOCK <= LANES - BIAS_LANE0
    assert depth == 1
    t = batch * seq
    x2 = x.reshape(t, d)
    c8 = jnp.pad(c, ((0, 8 - batch), (0, 0)))
    cos_t, sa_t, sb_t = _rope_tables(seq)
    bsel = _block_mean_rows(seq)
    row1 = lambda v: v.reshape(1, -1)
    per_batch = lambda v: v.reshape(batch, 1, d)

    for l in range(depth):
        mod = _ada(c8, w_ada[l], row1(b_ada[l]))[:batch]
        shift1, scale1, gate1, shift2, scale2, gate2 = [per_batch(m) for m in
                                                        jnp.split(mod, 6, axis=-1)]
        wl = w_in[l]
        w_p = jnp.concatenate(
            [wl[:, :POOL_WIDTH],
             _pad_heads(wl[:, POOL_WIDTH:POOL_WIDTH + ATTN_WIDTH]),
             _pad_heads(wl[:, POOL_WIDTH + ATTN_WIDTH:POOL_WIDTH + 2 * ATTN_WIDTH]),
             _pad_heads(wl[:, POOL_WIDTH + 2 * ATTN_WIDTH:POOL_WIDTH + 3 * ATTN_WIDTH]),
             wl[:, POOL_WIDTH + 3 * ATTN_WIDTH:]], axis=1).astype(BF16)
        u_pool, qp, kp, vp, gates = _inproj(x2, shift1, scale1, row1(norm_mix_g[l]), w_p,
                                            cos_t, sa_t, sb_t, seq=seq, tm=TILES.proj_tokens)
        attn = _moba(qp, kp, vp, bsel, batch=batch, seq=seq, group=TILES.moba_group,
                     heads=TILES.moba_heads)
        wa_p = _pad_heads(w_branch_attn[l].T).T.astype(BF16)
        x1, h2t = _merge(u_pool, attn, gates, x2, pool_w[l].astype(BF16), row1(pool_scale[l]),
                         w_branch_pool[l].astype(BF16), wa_p, w_out[l].astype(BF16), gate1,
                         row1(norm_ffn_g[l]), shift2, scale2, seq=seq, tm=TILES.proj_tokens)
        keys = peer_sub_keys[l].reshape(2 * PEER_HEADS, PEER_NKEYS, PEER_HALF)
        r2, e1, n, cden = _route(h2t, peer_wq[l].T.astype(BF16), keys, tm=TILES.route_tokens)
        x2 = _experts(h2t, peer_u[l], peer_v[l], r2, e1, n, cden, x1, gate2,
                      row1(norm_final_g), seq=seq, tm=TILES.expert_tokens,
                      sub_blocks=TILES.expert_sub_blocks)
    return x2.reshape(batch, seq, d)
```
